```python
import math
import jax, jax.numpy as jnp
from jax import lax
import numpy as np

D_MODEL = 1024
BATCH = 4
SEQ = 4096
DEPTH = 2

SB_HEADS = 8
SB_HEAD_DIM = 64
D_SB = SB_HEADS * SB_HEAD_DIM
MLA_HEADS = 8
MLA_NOPE_DIM = 64
MLA_ROPE_DIM = 32
MLA_QK_DIM = MLA_NOPE_DIM + MLA_ROPE_DIM
MLA_V_DIM = 64
MLA_Q_RANK = 256
MLA_KV_RANK = 128
D_MLA_OUT = MLA_HEADS * MLA_V_DIM
ROPE_THETA = 10000.0
D_MIX = D_SB + D_MLA_OUT
D_IN = 3 * D_SB + MLA_Q_RANK + MLA_KV_RANK + MLA_ROPE_DIM
Q_BLOCK = 128
PEER_HEADS = 8
PEER_N_KEYS = 128
PEER_N_EXPERTS = PEER_N_KEYS * PEER_N_KEYS
PEER_KEY_DIM = 256
PEER_HALF = PEER_KEY_DIM // 2
PEER_TOPK = 16
TOKEN_CHUNK = 128
DEEPNORM_ALPHA = (2.0 * DEPTH) ** 0.25
DEEPNORM_BETA = (8.0 * DEPTH) ** -0.25
LN_EPS = 1e-5
RMS_EPS = 1e-6

kernel_name = "hymba_sb_mla_peer_deepnorm"


def layer_norm(x, g, b):
    xf = x.astype(jnp.float32)
    mu = jnp.mean(xf, axis=-1, keepdims=True)
    var = jnp.mean(jnp.square(xf - mu), axis=-1, keepdims=True)
    return ((xf - mu) * lax.rsqrt(var + LN_EPS) * g + b).astype(x.dtype)


def rms_norm(x, g):
    xf = x.astype(jnp.float32)
    return (xf * lax.rsqrt(jnp.mean(jnp.square(xf), axis=-1, keepdims=True) + RMS_EPS) * g).astype(x.dtype)


def apply_rope(x, cos, sin):
    half = x.shape[-1] // 2
    x1, x2 = x[..., :half], x[..., half:]
    return jnp.concatenate([x1 * cos - x2 * sin, x2 * cos + x1 * sin], axis=-1).astype(x.dtype)


def stick_breaking_attention(q, k, v):
    S, d = q.shape[2], q.shape[3]
    scale = 1.0 / math.sqrt(d)
    outs = []
    for i in range(S // Q_BLOCK):
        q0, end = i * Q_BLOCK, (i + 1) * Q_BLOCK
        z = jnp.einsum('bhqd,bhkd->bhqk', q[:, :, q0:end], k[:, :, :end]).astype(jnp.float32) * scale
        t_idx = q0 + jnp.arange(Q_BLOCK)[:, None]
        s_idx = jnp.arange(end)[None, :]
        causal = s_idx < t_idx
        log_1m = jnp.where(causal, jax.nn.log_sigmoid(-z), 0.0)
        tail = lax.cumsum(log_1m, axis=3, reverse=True) - log_1m
        w = jnp.where(causal, jnp.exp(jax.nn.log_sigmoid(z) + tail), 0.0)
        outs.append(jnp.einsum('bhqk,bhkd->bhqd', w.astype(v.dtype), v[:, :, :end]))
    return jnp.concatenate(outs, axis=2)


def causal_softmax_attention(q, k, v):
    S, d = q.shape[2], q.shape[3]
    scale = 1.0 / math.sqrt(d)
    outs = []
    for i in range(S // Q_BLOCK):
        q0, end = i * Q_BLOCK, (i + 1) * Q_BLOCK
        s = jnp.einsum('bhqd,bhkd->bhqk', q[:, :, q0:end], k[:, :, :end]).astype(jnp.float32) * scale
        causal = jnp.arange(end)[None, :] <= (q0 + jnp.arange(Q_BLOCK)[:, None])
        p = jax.nn.softmax(jnp.where(causal, s, -jnp.inf), axis=-1)
        outs.append(jnp.einsum('bhqk,bhkd->bhqd', p.astype(v.dtype), v[:, :, :end]))
    return jnp.concatenate(outs, axis=2)


def hybrid_mixer(x, w_in, g_cq, w_uq, g_ckv, w_ukv, w_o, cos, sin):
    B, S, _ = x.shape
    h = x @ w_in
    cuts = [D_SB, 2 * D_SB, 3 * D_SB, 3 * D_SB + MLA_Q_RANK, 3 * D_SB + MLA_Q_RANK + MLA_KV_RANK]
    sb_q, sb_k, sb_v, c_q, c_kv, k_r = jnp.split(h, cuts, axis=-1)

    def heads(t, n):
        return t.reshape(B, S, n, -1).transpose(0, 2, 1, 3)

    o_sb = stick_breaking_attention(heads(sb_q, SB_HEADS), heads(sb_k, SB_HEADS), heads(sb_v, SB_HEADS))
    o_sb = o_sb.transpose(0, 2, 1, 3).reshape(B, S, D_SB)

    q = (rms_norm(c_q, g_cq) @ w_uq).reshape(B, S, MLA_HEADS, MLA_QK_DIM)
    q_nope, q_rope = q[..., :MLA_NOPE_DIM], q[..., MLA_NOPE_DIM:]
    q_rope = apply_rope(q_rope, cos[:, None, :], sin[:, None, :])
    k_rope = apply_rope(k_r, cos, sin)
    kv = (rms_norm(c_kv, g_ckv) @ w_ukv).reshape(B, S, MLA_HEADS, MLA_NOPE_DIM + MLA_V_DIM)
    k_nope, v_mla = kv[..., :MLA_NOPE_DIM], kv[..., MLA_NOPE_DIM:]
    q_full = jnp.concatenate([q_nope, q_rope], axis=-1)
    k_full = jnp.concatenate([k_nope, jnp.broadcast_to(k_rope[:, :, None, :], (B, S, MLA_HEADS, MLA_ROPE_DIM))], axis=-1)
    o_mla = causal_softmax_attention(q_full.transpose(0, 2, 1, 3), k_full.transpose(0, 2, 1, 3), v_mla.transpose(0, 2, 1, 3))
    o_mla = o_mla.transpose(0, 2, 1, 3).reshape(B, S, D_MLA_OUT)

    return jnp.concatenate([o_sb, o_mla], axis=-1) @ w_o


def peer_ffn(x, w_q, sub_keys, u_tab, v_tab):
    B, S, D = x.shape
    K = PEER_TOPK
    q = (x @ w_q).reshape(B, S, PEER_HEADS, 2, PEER_HALF)
    scores = jnp.einsum('bshcd,hcnd->bshcn', q, sub_keys).astype(jnp.float32)
    s_half, i_half = lax.top_k(scores, K)
    cand = (s_half[..., 0, :, None] + s_half[..., 1, None, :]).reshape(B, S, PEER_HEADS, K * K)
    s_top, c_idx = lax.top_k(cand, K)
    e_idx = (jnp.take_along_axis(i_half[..., 0, :], c_idx // K, axis=-1) * PEER_N_KEYS
             + jnp.take_along_axis(i_half[..., 1, :], c_idx % K, axis=-1))
    gate = jax.nn.softmax(s_top, axis=-1)
    n_chunks = (B * S) // TOKEN_CHUNK
    xs = x.reshape(n_chunks, TOKEN_CHUNK, D)
    idx = e_idx.reshape(n_chunks, TOKEN_CHUNK, PEER_HEADS * K)
    gs = gate.reshape(n_chunks, TOKEN_CHUNK, PEER_HEADS * K).astype(x.dtype)

    def chunk(args):
        xc, ic, gc = args
        u = jnp.take(u_tab, ic, axis=0)
        a = gc * jax.nn.gelu(jnp.einsum('td,tkd->tk', xc, u), approximate=False)
        v = jnp.take(v_tab, ic, axis=0)
        return jnp.einsum('tk,tkd->td', a, v)

    return lax.map(chunk, (xs, idx, gs)).reshape(B, S, D)


def setup_inputs(seed: int = 0) -> dict:
    key = jax.random.key(seed)
    ks = jax.random.split(key, 16)
    f32 = jnp.float32
    nrm = lambda k, shape: jax.random.normal(k, shape, f32)
    x = nrm(ks[0], (BATCH, SEQ, D_MODEL))
    col_scale = jnp.concatenate([jnp.ones((2 * D_SB,), f32), jnp.full((D_SB,), DEEPNORM_BETA, f32),
                                 jnp.ones((D_IN - 3 * D_SB,), f32)])
    w_in = nrm(ks[1], (DEPTH, D_MODEL, D_IN)) * D_MODEL ** -0.5 * col_scale
    g_cq = 1.0 + 0.02 * nrm(ks[2], (DEPTH, MLA_Q_RANK))
    w_uq = nrm(ks[3], (DEPTH, MLA_Q_RANK, MLA_HEADS * MLA_QK_DIM)) * MLA_Q_RANK ** -0.5
    g_ckv = 1.0 + 0.02 * nrm(ks[4], (DEPTH, MLA_KV_RANK))
    kv_scale = jnp.concatenate([jnp.ones((MLA_NOPE_DIM,), f32), jnp.full((MLA_V_DIM,), DEEPNORM_BETA, f32)])
    w_ukv = (nrm(ks[5], (DEPTH, MLA_KV_RANK, MLA_HEADS, MLA_NOPE_DIM + MLA_V_DIM)) * MLA_KV_RANK ** -0.5
             * kv_scale).reshape(DEPTH, MLA_KV_RANK, MLA_HEADS * (MLA_NOPE_DIM + MLA_V_DIM))
    w_o = nrm(ks[6], (DEPTH, D_MIX, D_MODEL)) * D_MIX ** -0.5 * DEEPNORM_BETA
    ln1_g = 1.0 + 0.02 * nrm(ks[7], (DEPTH, D_MODEL))
    ln1_b = 0.02 * nrm(ks[8], (DEPTH, D_MODEL))
    peer_wq = nrm(ks[9], (DEPTH, D_MODEL, PEER_HEADS * PEER_KEY_DIM)) * D_MODEL ** -0.5
    peer_keys = nrm(ks[10], (DEPTH, PEER_HEADS, 2, PEER_N_KEYS, PEER_HALF)) * PEER_HALF ** -0.5
    peer_u = nrm(ks[11], (DEPTH, PEER_N_EXPERTS, D_MODEL)) * D_MODEL ** -0.5
    peer_v = nrm(ks[12], (DEPTH, PEER_N_EXPERTS, D_MODEL)) * DEEPNORM_BETA
    ln2_g = 1.0 + 0.02 * nrm(ks[13], (DEPTH, D_MODEL))
    ln2_b = 0.02 * nrm(ks[14], (DEPTH, D_MODEL))
    return {"x": x, "w_in": w_in, "g_cq": g_cq, "w_uq": w_uq, "g_ckv": g_ckv, "w_ukv": w_ukv,
            "w_o": w_o, "ln1_g": ln1_g, "ln1_b": ln1_b, "peer_wq": peer_wq, "peer_keys": peer_keys,
            "peer_u": peer_u, "peer_v": peer_v, "ln2_g": ln2_g, "ln2_b": ln2_b}


def reference(x, w_in, g_cq, w_uq, g_ckv, w_ukv, w_o, ln1_g, ln1_b, peer_wq, peer_keys,
              peer_u, peer_v, ln2_g, ln2_b):
    S = x.shape[1]
    pos = jnp.arange(S, dtype=jnp.float32)
    inv_freq = ROPE_THETA ** (-jnp.arange(0, MLA_ROPE_DIM, 2, dtype=jnp.float32) / MLA_ROPE_DIM)
    ang = pos[:, None] * inv_freq[None, :]
    cos, sin = jnp.cos(ang), jnp.sin(ang)
    for l in range(DEPTH):
        mix = hybrid_mixer(x, w_in[l], g_cq[l], w_uq[l], g_ckv[l], w_ukv[l], w_o[l], cos, sin)
        x = layer_norm(DEEPNORM_ALPHA * x + mix, ln1_g[l], ln1_b[l])
        ffn = peer_ffn(x, peer_wq[l], peer_keys[l], peer_u[l], peer_v[l])
        x = layer_norm(DEEPNORM_ALPHA * x + ffn, ln2_g[l], ln2_b[l])
    return x
```

```python
import functools
import math

import jax
import jax.numpy as jnp
import numpy as np
from jax import lax
from jax.experimental import pallas as pl
from jax.experimental.pallas import tpu as pltpu

D_MODEL = 1024
DEPTH = 2
SB_HEADS = 8
SB_HEAD_DIM = 64
D_SB = SB_HEADS * SB_HEAD_DIM
MLA_HEADS = 8
MLA_NOPE_DIM = 64
MLA_ROPE_DIM = 32
MLA_QK_DIM = MLA_NOPE_DIM + MLA_ROPE_DIM
MLA_V_DIM = 64
MLA_Q_RANK = 256
MLA_KV_RANK = 128
D_MLA_OUT = MLA_HEADS * MLA_V_DIM
ROPE_THETA = 10000.0
PEER_HEADS = 8
PEER_N_KEYS = 128
PEER_N_EXPERTS = PEER_N_KEYS * PEER_N_KEYS
PEER_HALF = 128
PEER_TOPK = 16
DEEPNORM_ALPHA = (2.0 * DEPTH) ** 0.25
LN_EPS = 1e-5
RMS_EPS = 1e-6

LANES = 128
SUBLANES = 8
VMEM_LIMIT_BYTES = 56 * 1024 * 1024

TOKEN_TILE = 512
ATTN_TILE = 256
ROUTE_TILE = 256
DENSE_TOKEN_TILE = 256
DENSE_EXPERT_TILE = 512
GATE_ROWS = 16

NOT_RANKED = 4096.0
N_CAND = 56

F32 = jnp.float32
BF16 = jnp.bfloat16


def _dot(a, b):
    return jnp.dot(a, b, preferred_element_type=F32)


def _dot_nt(a, b):
    return lax.dot_general(a, b, (((1,), (1,)), ((), ())), preferred_element_type=F32)


def _params(semantics):
    return pltpu.CompilerParams(dimension_semantics=semantics, vmem_limit_bytes=VMEM_LIMIT_BYTES)


def _in_proj_kernel(x_ref, wsb_ref, wc_ref, wkr_ref, gq_ref, gkv_ref, wq2_ref, wkv_ref, cos_ref, sin_ref,
                    sbq_ref, sbk_ref, sbv_ref, mq_ref, mk_ref, mv_ref):
    xb = x_ref[...].astype(BF16)
    sb = _dot(xb, wsb_ref[...])
    sbq_ref[...] = sb[:, :D_SB].astype(BF16)
    sbk_ref[...] = sb[:, D_SB:2 * D_SB].astype(BF16)
    sbv_ref[...] = sb[:, 2 * D_SB:].astype(BF16)

    c = _dot(xb, wc_ref[...])
    cq = c[:, :MLA_Q_RANK]
    ckv = c[:, MLA_Q_RANK:]
    nq = cq * lax.rsqrt(jnp.mean(cq * cq, axis=-1, keepdims=True) + RMS_EPS) * gq_ref[...]
    nkv = ckv * lax.rsqrt(jnp.mean(ckv * ckv, axis=-1, keepdims=True) + RMS_EPS) * gkv_ref[...]

    cos = cos_ref[...]
    sin = sin_ref[...]
    kr = _dot(xb, wkr_ref[...])
    k_rope = kr[:, :LANES] * cos + kr[:, LANES:] * sin

    q2 = _dot(nq.astype(BF16), wq2_ref[...])
    kv = _dot(nkv.astype(BF16), wkv_ref[...])
    scale = 1.0 / math.sqrt(MLA_QK_DIM)
    for h in range(MLA_HEADS):
        lo, hi = h * LANES, (h + 1) * LANES
        qa = q2[:, lo:hi]
        qb = q2[:, MLA_HEADS * LANES + lo:MLA_HEADS * LANES + hi]
        mq_ref[:, lo:hi] = ((qa * cos + qb * sin) * scale).astype(BF16)
        mk_ref[:, lo:hi] = (kv[:, lo:hi] + k_rope).astype(BF16)
    mv_ref[...] = kv[:, MLA_HEADS * LANES:].astype(BF16)


def _in_proj(x2d, seq_len, wsb, wc, wkr, gq, gkv, wq2, wkv, cos_tab, sin_tab):
    n = x2d.shape[0]
    tm = min(TOKEN_TILE, seq_len)
    pos_tiles = seq_len // tm
    full = lambda a: pl.BlockSpec(a.shape, lambda i: (0,) * a.ndim)
    row = lambda w: pl.BlockSpec((tm, w), lambda i: (i, 0))
    pos = pl.BlockSpec((tm, LANES), lambda i: (i % pos_tiles, 0))
    outs = [(n, D_SB)] * 3 + [(n, MLA_HEADS * LANES)] * 2 + [(n, D_MLA_OUT)]
    return pl.pallas_call(
        _in_proj_kernel,
        grid=(n // tm,),
        in_specs=[row(D_MODEL), full(wsb), full(wc), full(wkr), full(gq), full(gkv), full(wq2), full(wkv), pos, pos],
        out_specs=[row(s[1]) for s in outs],
        out_shape=[jax.ShapeDtypeStruct(s, BF16) for s in outs],
        compiler_params=_params(("parallel",)),
        name="in_proj",
    )(x2d, wsb, wc, wkr, gq, gkv, wq2, wkv, cos_tab, sin_tab)


def _sb_attn_kernel(q_ref, k_ref, v_ref, u_ref, o_ref, acc_ref, c_ref, *, tile):
    qi = pl.program_id(2)
    q2 = q_ref[0]
    lane = lax.broadcasted_iota(jnp.int32, (tile, LANES), 1)
    zero = jnp.zeros_like(q2)
    q_heads = (jnp.where(lane < SB_HEAD_DIM, q2, zero), jnp.where(lane >= SB_HEAD_DIM, q2, zero))
    u = u_ref[...]
    r_id = lax.broadcasted_iota(jnp.int32, (tile, tile), 0)
    c_id = lax.broadcasted_iota(jnp.int32, (tile, tile), 1)
    causal = c_id < r_id

    acc_ref[...] = jnp.zeros_like(acc_ref)
    c_ref[...] = jnp.zeros_like(c_ref)

    def block(j, masked):
        start = pl.multiple_of(j * tile, tile)
        kb = k_ref[0, pl.ds(start, tile), :]
        vb = v_ref[0, pl.ds(start, tile), :]
        for hh in range(2):
            z = _dot_nt(q_heads[hh], kb)
            softplus = jnp.maximum(z, 0.0) + jnp.log1p(jnp.exp(-jnp.abs(z)))
            log_beta = z - softplus
            log_1m = -softplus
            if masked:
                log_1m = jnp.where(causal, log_1m, 0.0)
            hi = log_1m.astype(BF16)
            lo = (log_1m - hi.astype(F32)).astype(BF16)
            tail = _dot(hi, u) + _dot(lo, u) + c_ref[hh]
            w = jnp.exp(log_beta + tail)
            if masked:
                w = jnp.where(causal, w, 0.0)
            acc_ref[hh] += _dot(w.astype(BF16), vb)
            c_ref[hh] += jnp.sum(log_1m, axis=1, keepdims=True)

    block(qi, True)

    def body(t, carry):
        block(qi - 1 - t, False)
        return carry

    lax.fori_loop(0, qi, body, 0)
    o_ref[0] = jnp.where(lane < SB_HEAD_DIM, acc_ref[0], acc_ref[1]).astype(o_ref.dtype)


def _sb_attention(q, k, v, u):
    b, s, _ = q.shape
    tile = min(ATTN_TILE, s)
    pairs = D_SB // LANES
    q_spec = pl.BlockSpec((1, tile, LANES), lambda bi, p, i: (bi, i, p))
    kv_spec = pl.BlockSpec((1, s, LANES), lambda bi, p, i: (bi, 0, p))
    return pl.pallas_call(
        functools.partial(_sb_attn_kernel, tile=tile),
        grid=(b, pairs, s // tile),
        in_specs=[q_spec, kv_spec, kv_spec, pl.BlockSpec((tile, tile), lambda bi, p, i: (0, 0))],
        out_specs=q_spec,
        out_shape=jax.ShapeDtypeStruct((b, s, D_SB), BF16),
        scratch_shapes=[pltpu.VMEM((2, tile, LANES), F32), pltpu.VMEM((2, tile, 1), F32)],
        compiler_params=_params(("parallel", "parallel", "arbitrary")),
        name="sb_attn",
    )(q, k, v, u)


def _mla_attn_kernel(q_ref, k_ref, v_ref, o_ref, m_ref, l_ref, acc_ref, *, tile):
    qi = pl.program_id(2)
    q2 = q_ref[0]
    r_id = lax.broadcasted_iota(jnp.int32, (tile, tile), 0)
    c_id = lax.broadcasted_iota(jnp.int32, (tile, tile), 1)
    causal = c_id <= r_id

    m_ref[...] = jnp.full_like(m_ref, -jnp.inf)
    l_ref[...] = jnp.zeros_like(l_ref)
    acc_ref[...] = jnp.zeros_like(acc_ref)

    def block(j, masked):
        start = pl.multiple_of(j * tile, tile)
        kb = k_ref[0, pl.ds(start, tile), :]
        vb = v_ref[0, pl.ds(start, tile), :]
        for hh in range(2):
            s = _dot_nt(q2[:, hh * LANES:(hh + 1) * LANES], kb[:, hh * LANES:(hh + 1) * LANES])
            if masked:
                s = jnp.where(causal, s, -jnp.inf)
            m_old = m_ref[hh]
            m_new = jnp.maximum(m_old, jnp.max(s, axis=1, keepdims=True))
            p = jnp.exp(s - m_new)
            alpha = jnp.exp(m_old - m_new)
            l_ref[hh] = alpha * l_ref[hh] + jnp.sum(p, axis=1, keepdims=True)
            acc_ref[hh] = alpha * acc_ref[hh] + _dot(p.astype(BF16), vb)
            m_ref[hh] = m_new

    block(qi, True)

    def body(j, carry):
        block(j, False)
        return carry

    lax.fori_loop(0, qi, body, 0)
    lane = lax.broadcasted_iota(jnp.int32, (tile, LANES), 1)
    o = jnp.where(lane < MLA_V_DIM, acc_ref[0] / l_ref[0], acc_ref[1] / l_ref[1])
    o_ref[0] = o.astype(o_ref.dtype)


def _mla_attention(q, k, v):
    b, s, _ = q.shape
    tile = min(ATTN_TILE, s)
    pairs = MLA_HEADS // 2
    q_spec = pl.BlockSpec((1, tile, 2 * LANES), lambda bi, p, i: (bi, i, p))
    k_spec = pl.BlockSpec((1, s, 2 * LANES), lambda bi, p, i: (bi, 0, p))
    v_spec = pl.BlockSpec((1, s, LANES), lambda bi, p, i: (bi, 0, p))
    o_spec = pl.BlockSpec((1, tile, LANES), lambda bi, p, i: (bi, i, p))
    return pl.pallas_call(
        functools.partial(_mla_attn_kernel, tile=tile),
        grid=(b, pairs, s // tile),
        in_specs=[q_spec, k_spec, v_spec],
        out_specs=o_spec,
        out_shape=jax.ShapeDtypeStruct((b, s, D_MLA_OUT), BF16),
        scratch_shapes=[pltpu.VMEM((2, tile, 1), F32), pltpu.VMEM((2, tile, 1), F32), pltpu.VMEM((2, tile, LANES), F32)],
        compiler_params=_params(("parallel", "parallel", "arbitrary")),
        name="mla_attn",
    )(q, k, v)


def _layer_norm(y, g, b):
    mu = jnp.mean(y, axis=-1, keepdims=True)
    d = y - mu
    var = jnp.mean(d * d, axis=-1, keepdims=True)
    return d * lax.rsqrt(var + LN_EPS) * g + b


def _out_proj_kernel(osb_ref, omla_ref, x_ref, wo_ref, g_ref, b_ref, y_ref):
    mix = _dot(osb_ref[...], wo_ref[:D_SB, :]) + _dot(omla_ref[...], wo_ref[D_SB:, :])
    y_ref[...] = _layer_norm(DEEPNORM_ALPHA * x_ref[...] + mix, g_ref[...], b_ref[...])


def _out_proj_ln(osb, omla, x2d, wo, g, b):
    n = x2d.shape[0]
    tm = min(TOKEN_TILE, n)
    full = lambda a: pl.BlockSpec(a.shape, lambda i: (0,) * a.ndim)
    row = lambda w: pl.BlockSpec((tm, w), lambda i: (i, 0))
    return pl.pallas_call(
        _out_proj_kernel,
        grid=(n // tm,),
        in_specs=[row(D_SB), row(D_MLA_OUT), row(D_MODEL), full(wo), full(g), full(b)],
        out_specs=row(D_MODEL),
        out_shape=jax.ShapeDtypeStruct((n, D_MODEL), F32),
        compiler_params=_params(("parallel",)),
        name="out_proj_ln",
    )(osb, omla, x2d, wo, g, b)


def _candidate_tables():
    segs = [(a, PEER_TOPK // (a + 1)) for a in range(8)]
    ids = []
    for a, nb in segs:
        ids += [a * PEER_TOPK + b for b in range(nb)]
    ids += [a * PEER_TOPK for a in range(8, PEER_TOPK)]
    ids += [1000.0, 1001.0]
    ids += [2000.0 + i for i in range(N_CAND - len(ids))]
    return segs, np.asarray(ids, np.float32)


def _extract_top(s, n_take, n_rank, row_id, id_sentinel):
    rank = jnp.full(s.shape, NOT_RANKED, F32)
    vals, ids = [], []
    for r in range(n_take):
        m = jnp.max(s, axis=0, keepdims=True)
        i = jnp.min(jnp.where(s == m, row_id, id_sentinel), axis=0, keepdims=True)
        hit = row_id == i
        s = jnp.where(hit, -jnp.inf, s)
        if r < n_rank:
            rank = jnp.where(hit, float(r), rank)
        vals.append(m)
        ids.append(i)
    return vals, ids, rank


def _route_kernel(x_ref, wqt_ref, keys_ref, cid_ref,
                  xt_ref, sa_ref, sb_ref, e1_ref, e2_ref, ra_ref, rb_ref, tau_ref, ct_ref, flag_ref,
                  qt_ref, top_ref, cand_ref, *, tile, segs):
    xb = x_ref[...].astype(BF16)
    xt_ref[...] = xb.T
    qt_ref[...] = _dot_nt(wqt_ref[...], xb).astype(BF16)
    key_id = lax.broadcasted_iota(jnp.int32, (PEER_N_KEYS, LANES), 0).astype(F32)
    cand_id = cid_ref[...]

    def head(h, carry):
        for g in range(tile // LANES):
            lanes = slice(g * LANES, (g + 1) * LANES)
            tops = []
            for c in range(2):
                hc = h * 2 + c
                q = qt_ref[pl.ds(pl.multiple_of(hc * PEER_HALF, PEER_HALF), PEER_HALF), lanes]
                s = _dot(keys_ref[hc], q)
                vals, _, rank = _extract_top(s, PEER_TOPK + 1, PEER_TOPK, key_id, float(PEER_N_KEYS))
                rows = pl.ds(pl.multiple_of(h * PEER_N_KEYS, PEER_N_KEYS), PEER_N_KEYS)
                if c == 0:
                    sa_ref[rows, lanes] = s
                    ra_ref[rows, lanes] = (rank * float(PEER_TOPK)).astype(ra_ref.dtype)
                else:
                    sb_ref[rows, lanes] = s
                    rb_ref[rows, lanes] = rank.astype(rb_ref.dtype)
                for r, v in enumerate(vals):
                    top_ref[c, r:r + 1, :] = v
                tops.append((s, vals[0]))
            off = 0
            for a, nb in segs:
                cand_ref[off:off + nb, :] = top_ref[0, a:a + 1, :] + top_ref[1, 0:nb, :]
                off += nb
            cand_ref[off:off + 8, :] = top_ref[0, 8:16, :] + top_ref[1, 0:1, :]
            off += 8
            cand_ref[off:off + 1, :] = top_ref[0, 16:17, :] + top_ref[1, 0:1, :]
            cand_ref[off + 1:off + 2, :] = top_ref[0, 0:1, :] + top_ref[1, 16:17, :]
            cand_ref[off + 2:, :] = jnp.full((N_CAND - off - 2, LANES), -jnp.inf, F32)
            sums, cids, _ = _extract_top(cand_ref[...], PEER_TOPK + 1, 0, cand_id, 4000.0)
            z = jnp.zeros_like(sums[0])
            for r in range(PEER_TOPK):
                z = z + jnp.exp(sums[r] - sums[0])
            (s0, top0), (s1, top1) = tops
            rows = pl.ds(pl.multiple_of(h * PEER_N_KEYS, PEER_N_KEYS), PEER_N_KEYS)
            e1_ref[rows, lanes] = jnp.exp(s0 - top0) / z
            e2_ref[rows, lanes] = jnp.exp(s1 - top1)
            slab = pl.ds(pl.multiple_of(h * SUBLANES, SUBLANES), SUBLANES)
            tied = jnp.where(sums[PEER_TOPK] == sums[PEER_TOPK - 1], 1.0, 0.0)
            tau_ref[slab, lanes] = jnp.broadcast_to(sums[PEER_TOPK - 1], (SUBLANES, LANES))
            ct_ref[slab, lanes] = jnp.broadcast_to(cids[PEER_TOPK - 1], (SUBLANES, LANES))
            flag_ref[slab, lanes] = jnp.broadcast_to(tied, (SUBLANES, LANES))
        return carry

    lax.fori_loop(0, PEER_HEADS, head, 0)


def _peer_route(x2d, wqt, keys, cand_ids):
    n = x2d.shape[0]
    tile = min(ROUTE_TILE, n)
    segs, _ = _candidate_tables()
    hk = PEER_HEADS * PEER_N_KEYS
    full = lambda a: pl.BlockSpec(a.shape, lambda i: (0,) * a.ndim)
    col = lambda r: pl.BlockSpec((r, tile), lambda i: (0, i))
    outs = [((D_MODEL, n), BF16)] + [((hk, n), F32)] * 5 + [((hk, n), BF16)] + [((PEER_HEADS * SUBLANES, n), F32)] * 3
    return pl.pallas_call(
        functools.partial(_route_kernel, tile=tile, segs=segs),
        grid=(n // tile,),
        in_specs=[pl.BlockSpec((tile, D_MODEL), lambda i: (i, 0)), full(wqt), full(keys), full(cand_ids)],
        out_specs=[col(s[0]) for s, _ in outs],
        out_shape=[jax.ShapeDtypeStruct(s, d) for s, d in outs],
        scratch_shapes=[pltpu.VMEM((2 * PEER_HEADS * PEER_HALF, tile), BF16),
                        pltpu.VMEM((2, 24, LANES), F32),
                        pltpu.VMEM((N_CAND, LANES), F32)],
        compiler_params=_params(("parallel",)),
        name="peer_route",
    )(x2d, wqt, keys, cand_ids)


def _gelu(h):
    return 0.5 * h * (1.0 + lax.erf(h * (1.0 / math.sqrt(2.0))))


def _dense_kernel(xt_ref, u_ref, vt_ref, sa_ref, sb_ref, e1_ref, e2_ref, ra_ref, rb_ref, tau_ref, ct_ref, flag_ref,
                  x_ref, g_ref, b_ref, y_ref, acc_ref, h_ref, w_ref, *, tile, etile):
    e = pl.program_id(1)

    @pl.when(e == 0)
    def _():
        acc_ref[...] = jnp.zeros_like(acc_ref)

    h_ref[...] = _dot(u_ref[...], xt_ref[...])
    tied = jnp.max(flag_ref[...]) > 0.0

    def gates(exact):
        for r in range(etile // PEER_N_KEYS):
            i0 = e * (etile // PEER_N_KEYS) + r
            a_rows, e1_rows, ra_rows = [], [], []
            for h in range(PEER_HEADS):
                a_rows.append(sa_ref[pl.ds(h * PEER_N_KEYS + i0, 1), :])
                e1_rows.append(e1_ref[pl.ds(h * PEER_N_KEYS + i0, 1), :])
                if exact:
                    ra_rows.append(ra_ref[pl.ds(h * PEER_N_KEYS + i0, 1), :])
            for jc in range(PEER_N_KEYS // GATE_ROWS):
                gate = jnp.zeros((GATE_ROWS, tile), F32)
                for h in range(PEER_HEADS):
                    rows = slice(h * PEER_N_KEYS + jc * GATE_ROWS, h * PEER_N_KEYS + (jc + 1) * GATE_ROWS)
                    total = a_rows[h] + sb_ref[rows, :]
                    tau = tau_ref[h * SUBLANES:h * SUBLANES + 1, :]
                    if exact:
                        cand = ra_rows[h] + rb_ref[rows, :].astype(F32)
                        keep = (total > tau) | ((total == tau) & (cand <= ct_ref[h * SUBLANES:h * SUBLANES + 1, :]))
                    else:
                        keep = total >= tau
                    gate = gate + jnp.where(keep, e1_rows[h] * e2_ref[rows, :], 0.0)
                rows = slice(r * PEER_N_KEYS + jc * GATE_ROWS, r * PEER_N_KEYS + (jc + 1) * GATE_ROWS)
                w_ref[rows, :] = (gate * _gelu(h_ref[rows, :])).astype(BF16)

    @pl.when(jnp.logical_not(tied))
    def _():
        gates(False)

    @pl.when(tied)
    def _():
        gates(True)

    acc_ref[...] += _dot(vt_ref[...], w_ref[...])

    @pl.when(e == pl.num_programs(1) - 1)
    def _():
        y = DEEPNORM_ALPHA * x_ref[...] + acc_ref[...].T
        y_ref[...] = _layer_norm(y, g_ref[...], b_ref[...])


def _peer_dense(xt, u, vt, route, x2d, g, b):
    n = x2d.shape[0]
    tile = min(DENSE_TOKEN_TILE, n)
    etile = DENSE_EXPERT_TILE
    hk = PEER_HEADS * PEER_N_KEYS
    col = lambda r: pl.BlockSpec((r, tile), lambda i, e: (0, i))
    full = lambda a: pl.BlockSpec(a.shape, lambda i, e: (0,) * a.ndim)
    return pl.pallas_call(
        functools.partial(_dense_kernel, tile=tile, etile=etile),
        grid=(n // tile, PEER_N_EXPERTS // etile),
        in_specs=[col(D_MODEL),
                  pl.BlockSpec((etile, D_MODEL), lambda i, e: (e, 0)),
                  pl.BlockSpec((D_MODEL, etile), lambda i, e: (0, e))]
                 + [col(hk)] * 6 + [col(PEER_HEADS * SUBLANES)] * 3
                 + [pl.BlockSpec((tile, D_MODEL), lambda i, e: (i, 0)), full(g), full(b)],
        out_specs=pl.BlockSpec((tile, D_MODEL), lambda i, e: (i, 0)),
        out_shape=jax.ShapeDtypeStruct((n, D_MODEL), F32),
        scratch_shapes=[pltpu.VMEM((D_MODEL, tile), F32), pltpu.VMEM((etile, tile), F32), pltpu.VMEM((etile, tile), BF16)],
        compiler_params=_params(("parallel", "arbitrary")),
        name="peer_dense",
    )(xt, u, vt, *route, x2d, g, b)


def _rope_tables(seq_len):
    pos = jnp.arange(seq_len, dtype=F32)
    inv_freq = ROPE_THETA ** (-jnp.arange(0, MLA_ROPE_DIM, 2, dtype=F32) / MLA_ROPE_DIM)
    ang = pos[:, None] * inv_freq[None, :]
    cos, sin = jnp.cos(ang), jnp.sin(ang)
    ones = jnp.ones((seq_len, MLA_NOPE_DIM), F32)
    pad = jnp.zeros((seq_len, LANES - MLA_QK_DIM), F32)
    cos_tab = jnp.concatenate([ones, cos, cos, pad], axis=1)
    sin_tab = jnp.concatenate([0.0 * ones, -sin, sin, pad], axis=1)
    return cos_tab, sin_tab


def _rotate_half_columns(w):
    half = MLA_ROPE_DIM // 2
    return jnp.concatenate([w[..., half:], w[..., :half]], axis=-1)


def _mixer_weights(w_in, w_uq, w_ukv):
    sb_scale = 1.0 / math.sqrt(SB_HEAD_DIM)
    wsb = jnp.concatenate([w_in[:, :D_SB] * sb_scale, w_in[:, D_SB:3 * D_SB]], axis=1).astype(BF16)
    wc = w_in[:, 3 * D_SB:3 * D_SB + MLA_Q_RANK + MLA_KV_RANK].astype(BF16)
    w_kr = w_in[:, 3 * D_SB + MLA_Q_RANK + MLA_KV_RANK:]
    lead = jnp.zeros((D_MODEL, MLA_NOPE_DIM), F32)
    trail = jnp.zeros((D_MODEL, LANES - MLA_QK_DIM), F32)
    wkr = jnp.concatenate([lead, w_kr, trail, lead, _rotate_half_columns(w_kr), trail], axis=1).astype(BF16)

    wq = w_uq.reshape(MLA_Q_RANK, MLA_HEADS, MLA_QK_DIM)
    qpad = jnp.zeros((MLA_Q_RANK, MLA_HEADS, LANES - MLA_QK_DIM), F32)
    q_plain = jnp.concatenate([wq, qpad], axis=2)
    q_rot = jnp.concatenate([0.0 * wq[..., :MLA_NOPE_DIM], _rotate_half_columns(wq[..., MLA_NOPE_DIM:]), qpad], axis=2)
    wq2 = jnp.concatenate([q_plain.reshape(MLA_Q_RANK, -1), q_rot.reshape(MLA_Q_RANK, -1)], axis=1).astype(BF16)

    wkv = w_ukv.reshape(MLA_KV_RANK, MLA_HEADS, MLA_NOPE_DIM + MLA_V_DIM)
    kpad = jnp.zeros((MLA_KV_RANK, MLA_HEADS, LANES - MLA_NOPE_DIM), F32)
    wk = jnp.concatenate([wkv[..., :MLA_NOPE_DIM], kpad], axis=2).reshape(MLA_KV_RANK, -1)
    wv = wkv[..., MLA_NOPE_DIM:].reshape(MLA_KV_RANK, -1)
    wkv2 = jnp.concatenate([wk, wv], axis=1).astype(BF16)
    return wsb, wc, wkr, wq2, wkv2


def kernel(x, w_in, g_cq, w_uq, g_ckv, w_ukv, w_o, ln1_g, ln1_b, peer_wq, peer_keys, peer_u, peer_v, ln2_g, ln2_b):
    b, s, d = x.shape
    n = b * s
    cos_tab, sin_tab = _rope_tables(s)
    tile = min(ATTN_TILE, s)
    strict_upper = (jnp.arange(tile)[:, None] > jnp.arange(tile)[None, :]).astype(BF16)
    _, cand_ids = _candidate_tables()
    cand_ids = jnp.asarray(np.broadcast_to(cand_ids[:, None], (N_CAND, LANES)).copy())

    x2d = x.reshape(n, d)
    for l in range(DEPTH):
        wsb, wc, wkr, wq2, wkv2 = _mixer_weights(w_in[l], w_uq[l], w_ukv[l])
        sbq, sbk, sbv, mq, mk, mv = _in_proj(x2d, s, wsb, wc, wkr, g_cq[l][None, :], g_ckv[l][None, :], wq2, wkv2,
                                             cos_tab, sin_tab)
        sh = lambda a: a.reshape(b, s, a.shape[-1])
        o_sb = _sb_attention(sh(sbq), sh(sbk), sh(sbv), strict_upper)
        o_mla = _mla_attention(sh(mq), sh(mk), sh(mv))
        x2d = _out_proj_ln(o_sb.reshape(n, D_SB), o_mla.reshape(n, D_MLA_OUT), x2d, w_o[l].astype(BF16),
                           ln1_g[l][None, :], ln1_b[l][None, :])
        wqt = peer_wq[l].T.astype(BF16)
        keys = peer_keys[l].reshape(PEER_HEADS * 2, PEER_N_KEYS, PEER_HALF).astype(BF16)
        route = _peer_route(x2d, wqt, keys, cand_ids)
        x2d = _peer_dense(route[0], peer_u[l].astype(BF16), peer_v[l].T.astype(BF16), route[1:], x2d,
                          ln2_g[l][None, :], ln2_b[l][None, :])
    return x2d.reshape(b, s, d)
```

```python
import functools
import math

import jax
import jax.numpy as jnp
import numpy as np
from jax import lax
from jax.experimental import pallas as pl
from jax.experimental.pallas import tpu as pltpu

D_MODEL = 1024
DEPTH = 2
SB_HEADS = 8
SB_HEAD_DIM = 64
D_SB = SB_HEADS * SB_HEAD_DIM
MLA_HEADS = 8
MLA_NOPE_DIM = 64
MLA_ROPE_DIM = 32
MLA_QK_DIM = MLA_NOPE_DIM + MLA_ROPE_DIM
MLA_V_DIM = 64
MLA_Q_RANK = 256
MLA_KV_RANK = 128
D_MLA_OUT = MLA_HEADS * MLA_V_DIM
ROPE_THETA = 10000.0
PEER_HEADS = 8
PEER_N_KEYS = 128
PEER_N_EXPERTS = PEER_N_KEYS * PEER_N_KEYS
PEER_HALF = 128
PEER_TOPK = 16
DEEPNORM_ALPHA = (2.0 * DEPTH) ** 0.25
LN_EPS = 1e-5
RMS_EPS = 1e-6

LANES = 128
SUBLANES = 8
VMEM_LIMIT_BYTES = 56 * 1024 * 1024

TOKEN_TILE = 512
ATTN_TILE = 256
MLA_TILE = 512
ROUTE_TILE = 256
DENSE_TOKEN_TILE = 256
DENSE_EXPERT_TILE = 512
GATE_ROWS = 16

EXP_UNDERFLOW = -120.0
NOT_RANKED = 4096.0
N_CAND = 56

F32 = jnp.float32
BF16 = jnp.bfloat16


def _dot(a, b):
    return jnp.dot(a, b, preferred_element_type=F32)


def _dot_nt(a, b):
    return lax.dot_general(a, b, (((1,), (1,)), ((), ())), preferred_element_type=F32)


def _params(semantics):
    return pltpu.CompilerParams(dimension_semantics=semantics, vmem_limit_bytes=VMEM_LIMIT_BYTES)


def _in_proj_kernel(x_ref, wsb_ref, wc_ref, wkr_ref, gq_ref, gkv_ref, wq2_ref, wkv_ref, cos_ref, sin_ref,
                    sbq_ref, sbk_ref, sbv_ref, mq_ref, mk_ref, mv_ref):
    xb = x_ref[...].astype(BF16)
    sb = _dot(xb, wsb_ref[...])
    sbq_ref[...] = sb[:, :D_SB].astype(BF16)
    sbk_ref[...] = sb[:, D_SB:2 * D_SB].astype(BF16)
    sbv_ref[...] = sb[:, 2 * D_SB:].astype(BF16)

    c = _dot(xb, wc_ref[...])
    cq = c[:, :MLA_Q_RANK]
    ckv = c[:, MLA_Q_RANK:]
    nq = cq * lax.rsqrt(jnp.mean(cq * cq, axis=-1, keepdims=True) + RMS_EPS) * gq_ref[...]
    nkv = ckv * lax.rsqrt(jnp.mean(ckv * ckv, axis=-1, keepdims=True) + RMS_EPS) * gkv_ref[...]

    cos = cos_ref[...]
    sin = sin_ref[...]
    kr = _dot(xb, wkr_ref[...])
    k_rope = kr[:, :LANES] * cos + kr[:, LANES:] * sin

    q2 = _dot(nq.astype(BF16), wq2_ref[...])
    kv = _dot(nkv.astype(BF16), wkv_ref[...])
    scale = 1.0 / math.sqrt(MLA_QK_DIM)
    for h in range(MLA_HEADS):
        lo, hi = h * LANES, (h + 1) * LANES
        qa = q2[:, lo:hi]
        qb = q2[:, MLA_HEADS * LANES + lo:MLA_HEADS * LANES + hi]
        mq_ref[:, lo:hi] = ((qa * cos + qb * sin) * scale).astype(BF16)
        mk_ref[:, lo:hi] = (kv[:, lo:hi] + k_rope).astype(BF16)
    mv_ref[...] = kv[:, MLA_HEADS * LANES:].astype(BF16)


def _in_proj(x2d, seq_len, wsb, wc, wkr, gq, gkv, wq2, wkv, cos_tab, sin_tab):
    n = x2d.shape[0]
    tm = min(TOKEN_TILE, seq_len)
    pos_tiles = seq_len // tm
    full = lambda a: pl.BlockSpec(a.shape, lambda i: (0,) * a.ndim)
    row = lambda w: pl.BlockSpec((tm, w), lambda i: (i, 0))
    pos = pl.BlockSpec((tm, LANES), lambda i: (i % pos_tiles, 0))
    outs = [(n, D_SB)] * 3 + [(n, MLA_HEADS * LANES)] * 2 + [(n, D_MLA_OUT)]
    return pl.pallas_call(
        _in_proj_kernel,
        grid=(n // tm,),
        in_specs=[row(D_MODEL), full(wsb), full(wc), full(wkr), full(gq), full(gkv), full(wq2), full(wkv), pos, pos],
        out_specs=[row(s[1]) for s in outs],
        out_shape=[jax.ShapeDtypeStruct(s, BF16) for s in outs],
        compiler_params=_params(("parallel",)),
        name="in_proj",
    )(x2d, wsb, wc, wkr, gq, gkv, wq2, wkv, cos_tab, sin_tab)


def _sb_attn_kernel(q_ref, k_ref, v_ref, u_ref, o_ref, acc_ref, c_ref, *, tile):
    qi = pl.program_id(2)
    q2 = q_ref[0]
    lane = lax.broadcasted_iota(jnp.int32, (tile, LANES), 1)
    zero = jnp.zeros_like(q2)
    q_heads = (jnp.where(lane < SB_HEAD_DIM, q2, zero), jnp.where(lane >= SB_HEAD_DIM, q2, zero))
    u = u_ref[...]
    r_id = lax.broadcasted_iota(jnp.int32, (tile, tile), 0)
    c_id = lax.broadcasted_iota(jnp.int32, (tile, tile), 1)
    causal = c_id < r_id

    acc_ref[...] = jnp.zeros_like(acc_ref)
    c_ref[...] = jnp.zeros_like(c_ref)

    def block(j, masked):
        start = pl.multiple_of(j * tile, tile)
        kb = k_ref[0, pl.ds(start, tile), :]
        vb = v_ref[0, pl.ds(start, tile), :]
        for hh in range(2):
            z = _dot_nt(q_heads[hh], kb)
            softplus = jnp.maximum(z, 0.0) + jnp.log1p(jnp.exp(-jnp.abs(z)))
            log_beta = z - softplus
            log_1m = -softplus
            if masked:
                log_1m = jnp.where(causal, log_1m, 0.0)
            hi = log_1m.astype(BF16)
            lo = (log_1m - hi.astype(F32)).astype(BF16)
            tail = _dot(hi, u) + _dot(lo, u) + c_ref[hh]
            w = jnp.exp(log_beta + tail)
            if masked:
                w = jnp.where(causal, w, 0.0)
            acc_ref[hh] += _dot(w.astype(BF16), vb)
            c_ref[hh] += jnp.sum(log_1m, axis=1, keepdims=True)

    block(qi, True)

    def more(carry):
        t, c_max = carry
        return jnp.logical_and(t < qi, c_max > EXP_UNDERFLOW)

    def body(carry):
        t, _ = carry
        block(qi - 1 - t, False)
        return t + 1, jnp.max(c_ref[...])

    lax.while_loop(more, body, (jnp.int32(0), jnp.max(c_ref[...])))
    o_ref[0] = jnp.where(lane < SB_HEAD_DIM, acc_ref[0], acc_ref[1]).astype(o_ref.dtype)


def _sb_attention(q, k, v, u):
    b, s, _ = q.shape
    tile = min(ATTN_TILE, s)
    pairs = D_SB // LANES
    q_spec = pl.BlockSpec((1, tile, LANES), lambda bi, p, i: (bi, i, p))
    kv_spec = pl.BlockSpec((1, s, LANES), lambda bi, p, i: (bi, 0, p))
    return pl.pallas_call(
        functools.partial(_sb_attn_kernel, tile=tile),
        grid=(b, pairs, s // tile),
        in_specs=[q_spec, kv_spec, kv_spec, pl.BlockSpec((tile, tile), lambda bi, p, i: (0, 0))],
        out_specs=q_spec,
        out_shape=jax.ShapeDtypeStruct((b, s, D_SB), BF16),
        scratch_shapes=[pltpu.VMEM((2, tile, LANES), F32), pltpu.VMEM((2, tile, 1), F32)],
        compiler_params=_params(("parallel", "parallel", "arbitrary")),
        name="sb_attn",
    )(q, k, v, u)


def _mla_attn_kernel(q_ref, k_ref, v_ref, o_ref, m_ref, l_ref, acc_ref, *, tile):
    qi = pl.program_id(2)
    q2 = q_ref[0]
    r_id = lax.broadcasted_iota(jnp.int32, (tile, tile), 0)
    c_id = lax.broadcasted_iota(jnp.int32, (tile, tile), 1)
    causal = c_id <= r_id

    m_ref[...] = jnp.full_like(m_ref, -jnp.inf)
    l_ref[...] = jnp.zeros_like(l_ref)
    acc_ref[...] = jnp.zeros_like(acc_ref)

    def block(j, masked):
        start = pl.multiple_of(j * tile, tile)
        kb = k_ref[0, pl.ds(start, tile), :]
        vb = v_ref[0, pl.ds(start, tile), :]
        for hh in range(2):
            s = _dot_nt(q2[:, hh * LANES:(hh + 1) * LANES], kb[:, hh * LANES:(hh + 1) * LANES])
            if masked:
                s = jnp.where(causal, s, -jnp.inf)
            m_old = m_ref[hh]
            m_new = jnp.maximum(m_old, jnp.max(s, axis=1, keepdims=True))
            p = jnp.exp(s - m_new)
            alpha = jnp.exp(m_old - m_new)
            l_ref[hh] = alpha * l_ref[hh] + jnp.sum(p, axis=1, keepdims=True)
            acc_ref[hh] = alpha * acc_ref[hh] + _dot(p.astype(BF16), vb)
            m_ref[hh] = m_new

    block(qi, True)

    def body(j, carry):
        block(j, False)
        return carry

    lax.fori_loop(0, qi, body, 0)
    lane = lax.broadcasted_iota(jnp.int32, (tile, LANES), 1)
    o = jnp.where(lane < MLA_V_DIM, acc_ref[0] / l_ref[0], acc_ref[1] / l_ref[1])
    o_ref[0] = o.astype(o_ref.dtype)


def _mla_attention(q, k, v):
    b, s, _ = q.shape
    tile = min(MLA_TILE, s)
    pairs = MLA_HEADS // 2
    q_spec = pl.BlockSpec((1, tile, 2 * LANES), lambda bi, p, i: (bi, i, p))
    k_spec = pl.BlockSpec((1, s, 2 * LANES), lambda bi, p, i: (bi, 0, p))
    v_spec = pl.BlockSpec((1, s, LANES), lambda bi, p, i: (bi, 0, p))
    o_spec = pl.BlockSpec((1, tile, LANES), lambda bi, p, i: (bi, i, p))
    return pl.pallas_call(
        functools.partial(_mla_attn_kernel, tile=tile),
        grid=(b, pairs, s // tile),
        in_specs=[q_spec, k_spec, v_spec],
        out_specs=o_spec,
        out_shape=jax.ShapeDtypeStruct((b, s, D_MLA_OUT), BF16),
        scratch_shapes=[pltpu.VMEM((2, tile, 1), F32), pltpu.VMEM((2, tile, 1), F32), pltpu.VMEM((2, tile, LANES), F32)],
        compiler_params=_params(("parallel", "parallel", "arbitrary")),
        name="mla_attn",
    )(q, k, v)


def _layer_norm(y, g, b):
    mu = jnp.mean(y, axis=-1, keepdims=True)
    d = y - mu
    var = jnp.mean(d * d, axis=-1, keepdims=True)
    return d * lax.rsqrt(var + LN_EPS) * g + b


def _out_proj_kernel(osb_ref, omla_ref, x_ref, wo_ref, g_ref, b_ref, y_ref):
    mix = _dot(osb_ref[...], wo_ref[:D_SB, :]) + _dot(omla_ref[...], wo_ref[D_SB:, :])
    y_ref[...] = _layer_norm(DEEPNORM_ALPHA * x_ref[...] + mix, g_ref[...], b_ref[...])


def _out_proj_ln(osb, omla, x2d, wo, g, b):
    n = x2d.shape[0]
    tm = min(TOKEN_TILE, n)
    full = lambda a: pl.BlockSpec(a.shape, lambda i: (0,) * a.ndim)
    row = lambda w: pl.BlockSpec((tm, w), lambda i: (i, 0))
    return pl.pallas_call(
        _out_proj_kernel,
        grid=(n // tm,),
        in_specs=[row(D_SB), row(D_MLA_OUT), row(D_MODEL), full(wo), full(g), full(b)],
        out_specs=row(D_MODEL),
        out_shape=jax.ShapeDtypeStruct((n, D_MODEL), F32),
        compiler_params=_params(("parallel",)),
        name="out_proj_ln",
    )(osb, omla, x2d, wo, g, b)


def _candidate_tables():
    segs = [(a, PEER_TOPK // (a + 1)) for a in range(8)]
    ids = []
    for a, nb in segs:
        ids += [a * PEER_TOPK + b for b in range(nb)]
    ids += [a * PEER_TOPK for a in range(8, PEER_TOPK)]
    ids += [1000.0, 1001.0]
    ids += [2000.0 + i for i in range(N_CAND - len(ids))]
    return segs, np.asarray(ids, np.float32)


def _extract_top(s, n_take, n_rank, row_id, id_sentinel):
    rank = jnp.full(s.shape, NOT_RANKED, F32)
    vals, ids = [], []
    for r in range(n_take):
        m = jnp.max(s, axis=0, keepdims=True)
        i = jnp.min(jnp.where(s == m, row_id, id_sentinel), axis=0, keepdims=True)
        hit = row_id == i
        s = jnp.where(hit, -jnp.inf, s)
        if r < n_rank:
            rank = jnp.where(hit, float(r), rank)
        vals.append(m)
        ids.append(i)
    return vals, ids, rank


def _route_kernel(x_ref, wqt_ref, keys_ref, cid_ref,
                  xt_ref, sa_ref, sb_ref, e1_ref, e2_ref, ra_ref, rb_ref, tau_ref, ct_ref, flag_ref,
                  qt_ref, top_ref, cand_ref, *, tile, segs):
    xb = x_ref[...].astype(BF16)
    xt_ref[...] = xb.T
    qt_ref[...] = _dot_nt(wqt_ref[...], xb).astype(BF16)
    key_id = lax.broadcasted_iota(jnp.int32, (PEER_N_KEYS, LANES), 0).astype(F32)
    cand_id = cid_ref[...]

    def head(h, carry):
        for g in range(tile // LANES):
            lanes = slice(g * LANES, (g + 1) * LANES)
            tops = []
            for c in range(2):
                hc = h * 2 + c
                q = qt_ref[pl.ds(pl.multiple_of(hc * PEER_HALF, PEER_HALF), PEER_HALF), lanes]
                s = _dot(keys_ref[hc], q)
                vals, _, rank = _extract_top(s, PEER_TOPK + 1, PEER_TOPK, key_id, float(PEER_N_KEYS))
                rows = pl.ds(pl.multiple_of(h * PEER_N_KEYS, PEER_N_KEYS), PEER_N_KEYS)
                if c == 0:
                    sa_ref[g, rows, :] = s
                    ra_ref[g, rows, :] = (rank * float(PEER_TOPK)).astype(ra_ref.dtype)
                else:
                    sb_ref[g, rows, :] = s
                    rb_ref[g, rows, :] = rank.astype(rb_ref.dtype)
                for r, v in enumerate(vals):
                    top_ref[c, r:r + 1, :] = v
                tops.append((s, vals[0]))
            off = 0
            for a, nb in segs:
                cand_ref[off:off + nb, :] = top_ref[0, a:a + 1, :] + top_ref[1, 0:nb, :]
                off += nb
            cand_ref[off:off + 8, :] = top_ref[0, 8:16, :] + top_ref[1, 0:1, :]
            off += 8
            cand_ref[off:off + 1, :] = top_ref[0, 16:17, :] + top_ref[1, 0:1, :]
            cand_ref[off + 1:off + 2, :] = top_ref[0, 0:1, :] + top_ref[1, 16:17, :]
            cand_ref[off + 2:, :] = jnp.full((N_CAND - off - 2, LANES), -jnp.inf, F32)
            sums, cids, _ = _extract_top(cand_ref[...], PEER_TOPK + 1, 0, cand_id, 4000.0)
            z = jnp.zeros_like(sums[0])
            for r in range(PEER_TOPK):
                z = z + jnp.exp(sums[r] - sums[0])
            (s0, top0), (s1, top1) = tops
            rows = pl.ds(pl.multiple_of(h * PEER_N_KEYS, PEER_N_KEYS), PEER_N_KEYS)
            e1_ref[g, rows, :] = jnp.exp(s0 - top0) / z
            e2_ref[g, rows, :] = jnp.exp(s1 - top1)
            slab = pl.ds(pl.multiple_of(h * SUBLANES, SUBLANES), SUBLANES)
            tied = jnp.where(sums[PEER_TOPK] == sums[PEER_TOPK - 1], 1.0, 0.0)
            tau_ref[g, slab, :] = jnp.broadcast_to(sums[PEER_TOPK - 1], (SUBLANES, LANES))
            ct_ref[g, slab, :] = jnp.broadcast_to(cids[PEER_TOPK - 1], (SUBLANES, LANES))
            flag_ref[g, slab, :] = jnp.broadcast_to(tied, (SUBLANES, LANES))
        return carry

    lax.fori_loop(0, PEER_HEADS, head, 0)


def _peer_route(x2d, wqt, keys, cand_ids):
    n = x2d.shape[0]
    tile = min(ROUTE_TILE, n)
    segs, _ = _candidate_tables()
    hk = PEER_HEADS * PEER_N_KEYS
    full = lambda a: pl.BlockSpec(a.shape, lambda i: (0,) * a.ndim)
    groups = tile // LANES
    slab = lambda r: pl.BlockSpec((groups, r, LANES), lambda i: (i, 0, 0))
    slab_shape = lambda r: (n // LANES, r, LANES)
    hs = PEER_HEADS * SUBLANES
    outs = ([((D_MODEL, n), BF16)] + [(slab_shape(hk), F32)] * 5 + [(slab_shape(hk), BF16)]
            + [(slab_shape(hs), F32)] * 3)
    out_specs = ([pl.BlockSpec((D_MODEL, tile), lambda i: (0, i))] + [slab(hk)] * 6 + [slab(hs)] * 3)
    return pl.pallas_call(
        functools.partial(_route_kernel, tile=tile, segs=segs),
        grid=(n // tile,),
        in_specs=[pl.BlockSpec((tile, D_MODEL), lambda i: (i, 0)), full(wqt), full(keys), full(cand_ids)],
        out_specs=out_specs,
        out_shape=[jax.ShapeDtypeStruct(s, d) for s, d in outs],
        scratch_shapes=[pltpu.VMEM((2 * PEER_HEADS * PEER_HALF, tile), BF16),
                        pltpu.VMEM((2, 24, LANES), F32),
                        pltpu.VMEM((N_CAND, LANES), F32)],
        compiler_params=_params(("parallel",)),
        name="peer_route",
    )(x2d, wqt, keys, cand_ids)


def _gelu(h):
    return 0.5 * h * (1.0 + lax.erf(h * (1.0 / math.sqrt(2.0))))


def _dense_kernel(xt_ref, u_ref, vt_ref, sa_ref, sb_ref, e1_ref, e2_ref, ra_ref, rb_ref, tau_ref, ct_ref, flag_ref,
                  x_ref, g_ref, b_ref, y_ref, acc_ref, h_ref, w_ref, tied_ref, *, tile, etile):
    e = pl.program_id(1)
    last = pl.num_programs(1) - 1
    slot = lax.rem(e, 2)

    @pl.when(e == 0)
    def _():
        acc_ref[...] = jnp.zeros_like(acc_ref)
        w_ref[1] = jnp.zeros(w_ref.shape[1:], w_ref.dtype)
        tied_ref[0] = (jnp.max(flag_ref[...]) > 0.0).astype(jnp.int32)

    tied = tied_ref[0] > 0

    def step(exact):
        acc_ref[...] += _dot(vt_ref[...], w_ref[1 - slot])
        h_ref[...] = _dot(u_ref[...], xt_ref[...])
        for r in range(etile // PEER_N_KEYS):
            i0 = e * (etile // PEER_N_KEYS) + r
            for g in range(tile // LANES):
                lanes = slice(g * LANES, (g + 1) * LANES)
                row_of = lambda ref, h: jnp.broadcast_to(ref[g, pl.ds(h * PEER_N_KEYS + i0, 1), :], (SUBLANES, LANES))
                head_of = lambda ref, h: ref[g, h * SUBLANES:(h + 1) * SUBLANES, :]
                a_b = [row_of(sa_ref, h) for h in range(PEER_HEADS)]
                e1_b = [row_of(e1_ref, h) for h in range(PEER_HEADS)]
                tau_b = [head_of(tau_ref, h) for h in range(PEER_HEADS)]
                if exact:
                    ra_b = [row_of(ra_ref, h) for h in range(PEER_HEADS)]
                    ct_b = [head_of(ct_ref, h) for h in range(PEER_HEADS)]
                for jc in range(PEER_N_KEYS // GATE_ROWS):
                    if exact:
                        rb = [rb_ref[g, h * PEER_N_KEYS + jc * GATE_ROWS:h * PEER_N_KEYS + (jc + 1) * GATE_ROWS, :]
                              .astype(F32) for h in range(PEER_HEADS)]
                    parts = []
                    for sub in range(GATE_ROWS // SUBLANES):
                        j0 = jc * GATE_ROWS + sub * SUBLANES
                        gate = None
                        for h in range(PEER_HEADS):
                            rows = slice(h * PEER_N_KEYS + j0, h * PEER_N_KEYS + j0 + SUBLANES)
                            total = a_b[h] + sb_ref[g, rows, :]
                            if exact:
                                cand = ra_b[h] + rb[h][sub * SUBLANES:(sub + 1) * SUBLANES]
                                keep = (total > tau_b[h]) | ((total == tau_b[h]) & (cand <= ct_b[h]))
                            else:
                                keep = total >= tau_b[h]
                            term = jnp.where(keep, e1_b[h] * e2_ref[g, rows, :], 0.0)
                            gate = term if gate is None else gate + term
                        hv = h_ref[r * PEER_N_KEYS + j0:r * PEER_N_KEYS + j0 + SUBLANES, lanes]
                        parts.append(gate * _gelu(hv))
                    rows = slice(r * PEER_N_KEYS + jc * GATE_ROWS, r * PEER_N_KEYS + (jc + 1) * GATE_ROWS)
                    w_ref[slot, rows, lanes] = jnp.concatenate(parts, axis=0).astype(BF16)

    @pl.when(jnp.logical_and(e < last, jnp.logical_not(tied)))
    def _():
        step(False)

    @pl.when(jnp.logical_and(e < last, tied))
    def _():
        step(True)

    @pl.when(e == last)
    def _():
        ffn_t = acc_ref[...] + _dot(vt_ref[...], w_ref[1 - slot])
        y = DEEPNORM_ALPHA * x_ref[...] + ffn_t.T
        y_ref[...] = _layer_norm(y, g_ref[...], b_ref[...])


def _peer_dense(xt, u, vt, route, x2d, g, b):
    n = x2d.shape[0]
    tile = min(DENSE_TOKEN_TILE, n)
    etile = DENSE_EXPERT_TILE
    steps = PEER_N_EXPERTS // etile
    hk = PEER_HEADS * PEER_N_KEYS
    slab = lambda r: pl.BlockSpec((tile // LANES, r, LANES), lambda i, e: (i, 0, 0))
    full = lambda a: pl.BlockSpec(a.shape, lambda i, e: (0,) * a.ndim)
    return pl.pallas_call(
        functools.partial(_dense_kernel, tile=tile, etile=etile),
        grid=(n // tile, steps + 1),
        in_specs=[pl.BlockSpec((D_MODEL, tile), lambda i, e: (0, i)),
                  pl.BlockSpec((etile, D_MODEL), lambda i, e: (jnp.minimum(e, steps - 1), 0)),
                  pl.BlockSpec((D_MODEL, etile), lambda i, e: (0, jnp.maximum(e - 1, 0)))]
                 + [slab(hk)] * 6 + [slab(PEER_HEADS * SUBLANES)] * 3
                 + [pl.BlockSpec((tile, D_MODEL), lambda i, e: (i, 0)), full(g), full(b)],
        out_specs=pl.BlockSpec((tile, D_MODEL), lambda i, e: (i, 0)),
        out_shape=jax.ShapeDtypeStruct((n, D_MODEL), F32),
        scratch_shapes=[pltpu.VMEM((D_MODEL, tile), F32), pltpu.VMEM((etile, tile), F32),
                        pltpu.VMEM((2, etile, tile), BF16), pltpu.SMEM((1,), jnp.int32)],
        compiler_params=_params(("parallel", "arbitrary")),
        name="peer_dense",
    )(xt, u, vt, *route, x2d, g, b)


def _rope_tables(seq_len):
    pos = jnp.arange(seq_len, dtype=F32)
    inv_freq = ROPE_THETA ** (-jnp.arange(0, MLA_ROPE_DIM, 2, dtype=F32) / MLA_ROPE_DIM)
    ang = pos[:, None] * inv_freq[None, :]
    cos, sin = jnp.cos(ang), jnp.sin(ang)
    ones = jnp.ones((seq_len, MLA_NOPE_DIM), F32)
    pad = jnp.zeros((seq_len, LANES - MLA_QK_DIM), F32)
    cos_tab = jnp.concatenate([ones, cos, cos, pad], axis=1)
    sin_tab = jnp.concatenate([0.0 * ones, -sin, sin, pad], axis=1)
    return cos_tab, sin_tab


def _rotate_half_columns(w):
    half = MLA_ROPE_DIM // 2
    return jnp.concatenate([w[..., half:], w[..., :half]], axis=-1)


def _mixer_weights(w_in, w_uq, w_ukv):
    sb_scale = 1.0 / math.sqrt(SB_HEAD_DIM)
    wsb = jnp.concatenate([w_in[:, :D_SB] * sb_scale, w_in[:, D_SB:3 * D_SB]], axis=1).astype(BF16)
    wc = w_in[:, 3 * D_SB:3 * D_SB + MLA_Q_RANK + MLA_KV_RANK].astype(BF16)
    w_kr = w_in[:, 3 * D_SB + MLA_Q_RANK + MLA_KV_RANK:]
    lead = jnp.zeros((D_MODEL, MLA_NOPE_DIM), F32)
    trail = jnp.zeros((D_MODEL, LANES - MLA_QK_DIM), F32)
    wkr = jnp.concatenate([lead, w_kr, trail, lead, _rotate_half_columns(w_kr), trail], axis=1).astype(BF16)

    wq = w_uq.reshape(MLA_Q_RANK, MLA_HEADS, MLA_QK_DIM)
    qpad = jnp.zeros((MLA_Q_RANK, MLA_HEADS, LANES - MLA_QK_DIM), F32)
    q_plain = jnp.concatenate([wq, qpad], axis=2)
    q_rot = jnp.concatenate([0.0 * wq[..., :MLA_NOPE_DIM], _rotate_half_columns(wq[..., MLA_NOPE_DIM:]), qpad], axis=2)
    wq2 = jnp.concatenate([q_plain.reshape(MLA_Q_RANK, -1), q_rot.reshape(MLA_Q_RANK, -1)], axis=1).astype(BF16)

    wkv = w_ukv.reshape(MLA_KV_RANK, MLA_HEADS, MLA_NOPE_DIM + MLA_V_DIM)
    kpad = jnp.zeros((MLA_KV_RANK, MLA_HEADS, LANES - MLA_NOPE_DIM), F32)
    wk = jnp.concatenate([wkv[..., :MLA_NOPE_DIM], kpad], axis=2).reshape(MLA_KV_RANK, -1)
    wv = wkv[..., MLA_NOPE_DIM:].reshape(MLA_KV_RANK, -1)
    wkv2 = jnp.concatenate([wk, wv], axis=1).astype(BF16)
    return wsb, wc, wkr, wq2, wkv2


def kernel(x, w_in, g_cq, w_uq, g_ckv, w_ukv, w_o, ln1_g, ln1_b, peer_wq, peer_keys, peer_u, peer_v, ln2_g, ln2_b):
    b, s, d = x.shape
    n = b * s
    cos_tab, sin_tab = _rope_tables(s)
    tile = min(ATTN_TILE, s)
    strict_upper = (jnp.arange(tile)[:, None] > jnp.arange(tile)[None, :]).astype(BF16)
    _, cand_ids = _candidate_tables()
    cand_ids = jnp.asarray(np.broadcast_to(cand_ids[:, None], (N_CAND, LANES)).copy())

    x2d = x.reshape(n, d)
    for l in range(DEPTH):
        wsb, wc, wkr, wq2, wkv2 = _mixer_weights(w_in[l], w_uq[l], w_ukv[l])
        sbq, sbk, sbv, mq, mk, mv = _in_proj(x2d, s, wsb, wc, wkr, g_cq[l][None, :], g_ckv[l][None, :], wq2, wkv2,
                                             cos_tab, sin_tab)
        sh = lambda a: a.reshape(b, s, a.shape[-1])
        o_sb = _sb_attention(sh(sbq), sh(sbk), sh(sbv), strict_upper)
        o_mla = _mla_attention(sh(mq), sh(mk), sh(mv))
        x2d = _out_proj_ln(o_sb.reshape(n, D_SB), o_mla.reshape(n, D_MLA_OUT), x2d, w_o[l].astype(BF16),
                           ln1_g[l][None, :], ln1_b[l][None, :])
        wqt = peer_wq[l].T.astype(BF16)
        keys = peer_keys[l].reshape(PEER_HEADS * 2, PEER_N_KEYS, PEER_HALF).astype(BF16)
        route = _peer_route(x2d, wqt, keys, cand_ids)
        x2d = _peer_dense(route[0], peer_u[l].astype(BF16), peer_v[l].T.astype(BF16), route[1:], x2d,
                          ln2_g[l][None, :], ln2_b[l][None, :])
    return x2d.reshape(b, s, d)
```

```python
import functools
import math

import jax
import jax.numpy as jnp
import numpy as np
from jax import lax
from jax.experimental import pallas as pl
from jax.experimental.pallas import tpu as pltpu

D_MODEL = 1024
DEPTH = 2
SB_HEADS = 8
SB_HEAD_DIM = 64
D_SB = SB_HEADS * SB_HEAD_DIM
MLA_HEADS = 8
MLA_NOPE_DIM = 64
MLA_ROPE_DIM = 32
MLA_QK_DIM = MLA_NOPE_DIM + MLA_ROPE_DIM
MLA_V_DIM = 64
MLA_Q_RANK = 256
MLA_KV_RANK = 128
D_MLA_OUT = MLA_HEADS * MLA_V_DIM
ROPE_THETA = 10000.0
PEER_HEADS = 8
PEER_N_KEYS = 128
PEER_N_EXPERTS = PEER_N_KEYS * PEER_N_KEYS
PEER_HALF = 128
PEER_TOPK = 16
DEEPNORM_ALPHA = (2.0 * DEPTH) ** 0.25
LN_EPS = 1e-5
RMS_EPS = 1e-6

LANES = 128
SUBLANES = 8
VMEM_LIMIT_BYTES = 56 * 1024 * 1024

TOKEN_TILE = 512
ATTN_TILE = 256
MLA_TILE = 512
ROUTE_TILE = 512
DENSE_TOKEN_TILE = 256
DENSE_EXPERT_TILE = 512
GATE_ROWS = 16

EXP_UNDERFLOW = -120.0
NOT_RANKED = 4096.0
N_CAND = 56

F32 = jnp.float32
BF16 = jnp.bfloat16


def _dot(a, b):
    return jnp.dot(a, b, preferred_element_type=F32)


def _dot_nt(a, b):
    return lax.dot_general(a, b, (((1,), (1,)), ((), ())), preferred_element_type=F32)


def _params(semantics, flags=None):
    return pltpu.CompilerParams(dimension_semantics=semantics, vmem_limit_bytes=VMEM_LIMIT_BYTES, flags=flags)


def _in_proj_kernel(x_ref, wsb_ref, wc_ref, wkr_ref, gq_ref, gkv_ref, wq2_ref, wkv_ref, cos_ref, sin_ref,
                    sbq_ref, sbk_ref, sbv_ref, mq_ref, mk_ref, mv_ref):
    xb = x_ref[...].astype(BF16)
    sb = _dot(xb, wsb_ref[...])
    sbq_ref[...] = sb[:, :D_SB].astype(BF16)
    sbk_ref[...] = sb[:, D_SB:2 * D_SB].astype(BF16)
    sbv_ref[...] = sb[:, 2 * D_SB:].astype(BF16)

    c = _dot(xb, wc_ref[...])
    cq = c[:, :MLA_Q_RANK]
    ckv = c[:, MLA_Q_RANK:]
    nq = cq * lax.rsqrt(jnp.mean(cq * cq, axis=-1, keepdims=True) + RMS_EPS) * gq_ref[...]
    nkv = ckv * lax.rsqrt(jnp.mean(ckv * ckv, axis=-1, keepdims=True) + RMS_EPS) * gkv_ref[...]

    cos = cos_ref[...]
    sin = sin_ref[...]
    kr = _dot(xb, wkr_ref[...])
    k_rope = kr[:, :LANES] * cos + kr[:, LANES:] * sin

    q2 = _dot(nq.astype(BF16), wq2_ref[...])
    kv = _dot(nkv.astype(BF16), wkv_ref[...])
    scale = 1.0 / math.sqrt(MLA_QK_DIM)
    for h in range(MLA_HEADS):
        lo, hi = h * LANES, (h + 1) * LANES
        qa = q2[:, lo:hi]
        qb = q2[:, MLA_HEADS * LANES + lo:MLA_HEADS * LANES + hi]
        mq_ref[:, lo:hi] = ((qa * cos + qb * sin) * scale).astype(BF16)
        mk_ref[:, lo:hi] = (kv[:, lo:hi] + k_rope).astype(BF16)
    mv_ref[...] = kv[:, MLA_HEADS * LANES:].astype(BF16)


def _in_proj(x2d, seq_len, wsb, wc, wkr, gq, gkv, wq2, wkv, cos_tab, sin_tab):
    n = x2d.shape[0]
    tm = min(TOKEN_TILE, seq_len)
    pos_tiles = seq_len // tm
    full = lambda a: pl.BlockSpec(a.shape, lambda i: (0,) * a.ndim)
    row = lambda w: pl.BlockSpec((tm, w), lambda i: (i, 0))
    pos = pl.BlockSpec((tm, LANES), lambda i: (i % pos_tiles, 0))
    outs = [(n, D_SB)] * 3 + [(n, MLA_HEADS * LANES)] * 2 + [(n, D_MLA_OUT)]
    return pl.pallas_call(
        _in_proj_kernel,
        grid=(n // tm,),
        in_specs=[row(D_MODEL), full(wsb), full(wc), full(wkr), full(gq), full(gkv), full(wq2), full(wkv), pos, pos],
        out_specs=[row(s[1]) for s in outs],
        out_shape=[jax.ShapeDtypeStruct(s, BF16) for s in outs],
        compiler_params=_params(("parallel",)),
        name="in_proj",
    )(x2d, wsb, wc, wkr, gq, gkv, wq2, wkv, cos_tab, sin_tab)


def _sb_attn_kernel(q_ref, k_ref, v_ref, u_ref, o_ref, acc_ref, c_ref, *, tile):
    qi = pl.program_id(2)
    q2 = q_ref[0]
    lane = lax.broadcasted_iota(jnp.int32, (tile, LANES), 1)
    zero = jnp.zeros_like(q2)
    q_heads = (jnp.where(lane < SB_HEAD_DIM, q2, zero), jnp.where(lane >= SB_HEAD_DIM, q2, zero))
    u = u_ref[...]
    r_id = lax.broadcasted_iota(jnp.int32, (tile, tile), 0)
    c_id = lax.broadcasted_iota(jnp.int32, (tile, tile), 1)
    causal = c_id < r_id

    acc_ref[...] = jnp.zeros_like(acc_ref)
    c_ref[...] = jnp.zeros_like(c_ref)

    def block(j, masked):
        start = pl.multiple_of(j * tile, tile)
        kb = k_ref[0, pl.ds(start, tile), :]
        vb = v_ref[0, pl.ds(start, tile), :]
        for hh in range(2):
            z = _dot_nt(q_heads[hh], kb)
            softplus = jnp.maximum(z, 0.0) + jnp.log1p(jnp.exp(-jnp.abs(z)))
            log_beta = z - softplus
            log_1m = -softplus
            if masked:
                log_1m = jnp.where(causal, log_1m, 0.0)
            hi = log_1m.astype(BF16)
            lo = (log_1m - hi.astype(F32)).astype(BF16)
            tail = _dot(hi, u) + _dot(lo, u) + c_ref[hh]
            w = jnp.exp(log_beta + tail)
            if masked:
                w = jnp.where(causal, w, 0.0)
            acc_ref[hh] += _dot(w.astype(BF16), vb)
            c_ref[hh] += jnp.sum(log_1m, axis=1, keepdims=True)

    block(qi, True)

    def more(carry):
        t, c_max = carry
        return jnp.logical_and(t < qi, c_max > EXP_UNDERFLOW)

    def body(carry):
        t, _ = carry
        block(qi - 1 - t, False)
        return t + 1, jnp.max(c_ref[...])

    lax.while_loop(more, body, (jnp.int32(0), jnp.max(c_ref[...])))
    o_ref[0] = jnp.where(lane < SB_HEAD_DIM, acc_ref[0], acc_ref[1]).astype(o_ref.dtype)


def _sb_attention(q, k, v, u):
    b, s, _ = q.shape
    tile = min(ATTN_TILE, s)
    pairs = D_SB // LANES
    q_spec = pl.BlockSpec((1, tile, LANES), lambda bi, p, i: (bi, i, p))
    kv_spec = pl.BlockSpec((1, s, LANES), lambda bi, p, i: (bi, 0, p))
    return pl.pallas_call(
        functools.partial(_sb_attn_kernel, tile=tile),
        grid=(b, pairs, s // tile),
        in_specs=[q_spec, kv_spec, kv_spec, pl.BlockSpec((tile, tile), lambda bi, p, i: (0, 0))],
        out_specs=q_spec,
        out_shape=jax.ShapeDtypeStruct((b, s, D_SB), BF16),
        scratch_shapes=[pltpu.VMEM((2, tile, LANES), F32), pltpu.VMEM((2, tile, 1), F32)],
        compiler_params=_params(("parallel", "parallel", "arbitrary")),
        name="sb_attn",
    )(q, k, v, u)


def _mla_attn_kernel(q_ref, k_ref, v_ref, o_ref, m_ref, l_ref, acc_ref, *, tile):
    qi = pl.program_id(2)
    q2 = q_ref[0]
    r_id = lax.broadcasted_iota(jnp.int32, (tile, tile), 0)
    c_id = lax.broadcasted_iota(jnp.int32, (tile, tile), 1)
    causal = c_id <= r_id

    m_ref[...] = jnp.full_like(m_ref, -jnp.inf)
    l_ref[...] = jnp.zeros_like(l_ref)
    acc_ref[...] = jnp.zeros_like(acc_ref)

    def block(j, masked):
        start = pl.multiple_of(j * tile, tile)
        kb = k_ref[0, pl.ds(start, tile), :]
        vb = v_ref[0, pl.ds(start, tile), :]
        for hh in range(2):
            s = _dot_nt(q2[:, hh * LANES:(hh + 1) * LANES], kb[:, hh * LANES:(hh + 1) * LANES])
            if masked:
                s = jnp.where(causal, s, -jnp.inf)
            m_old = m_ref[hh]
            m_new = jnp.maximum(m_old, jnp.max(s, axis=1, keepdims=True))
            p = jnp.exp(s - m_new)
            alpha = jnp.exp(m_old - m_new)
            l_ref[hh] = alpha * l_ref[hh] + jnp.sum(p, axis=1, keepdims=True)
            acc_ref[hh] = alpha * acc_ref[hh] + _dot(p.astype(BF16), vb)
            m_ref[hh] = m_new

    block(qi, True)

    def body(j, carry):
        block(j, False)
        return carry

    lax.fori_loop(0, qi, body, 0)
    lane = lax.broadcasted_iota(jnp.int32, (tile, LANES), 1)
    o = jnp.where(lane < MLA_V_DIM, acc_ref[0] / l_ref[0], acc_ref[1] / l_ref[1])
    o_ref[0] = o.astype(o_ref.dtype)


def _mla_attention(q, k, v):
    b, s, _ = q.shape
    tile = min(MLA_TILE, s)
    pairs = MLA_HEADS // 2
    q_spec = pl.BlockSpec((1, tile, 2 * LANES), lambda bi, p, i: (bi, i, p))
    k_spec = pl.BlockSpec((1, s, 2 * LANES), lambda bi, p, i: (bi, 0, p))
    v_spec = pl.BlockSpec((1, s, LANES), lambda bi, p, i: (bi, 0, p))
    o_spec = pl.BlockSpec((1, tile, LANES), lambda bi, p, i: (bi, i, p))
    return pl.pallas_call(
        functools.partial(_mla_attn_kernel, tile=tile),
        grid=(b, pairs, s // tile),
        in_specs=[q_spec, k_spec, v_spec],
        out_specs=o_spec,
        out_shape=jax.ShapeDtypeStruct((b, s, D_MLA_OUT), BF16),
        scratch_shapes=[pltpu.VMEM((2, tile, 1), F32), pltpu.VMEM((2, tile, 1), F32), pltpu.VMEM((2, tile, LANES), F32)],
        compiler_params=_params(("parallel", "parallel", "arbitrary")),
        name="mla_attn",
    )(q, k, v)


def _layer_norm(y, g, b):
    mu = jnp.mean(y, axis=-1, keepdims=True)
    d = y - mu
    var = jnp.mean(d * d, axis=-1, keepdims=True)
    return d * lax.rsqrt(var + LN_EPS) * g + b


def _out_proj_kernel(osb_ref, omla_ref, x_ref, wo_ref, g_ref, b_ref, y_ref):
    mix = _dot(osb_ref[...], wo_ref[:D_SB, :]) + _dot(omla_ref[...], wo_ref[D_SB:, :])
    y_ref[...] = _layer_norm(DEEPNORM_ALPHA * x_ref[...] + mix, g_ref[...], b_ref[...])


def _out_proj_ln(osb, omla, x2d, wo, g, b):
    n = x2d.shape[0]
    tm = min(TOKEN_TILE, n)
    full = lambda a: pl.BlockSpec(a.shape, lambda i: (0,) * a.ndim)
    row = lambda w: pl.BlockSpec((tm, w), lambda i: (i, 0))
    return pl.pallas_call(
        _out_proj_kernel,
        grid=(n // tm,),
        in_specs=[row(D_SB), row(D_MLA_OUT), row(D_MODEL), full(wo), full(g), full(b)],
        out_specs=row(D_MODEL),
        out_shape=jax.ShapeDtypeStruct((n, D_MODEL), F32),
        compiler_params=_params(("parallel",)),
        name="out_proj_ln",
    )(osb, omla, x2d, wo, g, b)


def _candidate_tables():
    segs = [(a, PEER_TOPK // (a + 1)) for a in range(8)]
    ids = []
    for a, nb in segs:
        ids += [a * PEER_TOPK + b for b in range(nb)]
    ids += [a * PEER_TOPK for a in range(8, PEER_TOPK)]
    ids += [1000.0, 1001.0]
    ids += [2000.0 + i for i in range(N_CAND - len(ids))]
    return segs, np.asarray(ids, np.float32)


def _extract_top(s, n_take, n_rank, row_id, id_sentinel):
    rank = jnp.full(s.shape, NOT_RANKED, F32)
    vals, ids = [], []
    for r in range(n_take):
        m = jnp.max(s, axis=0, keepdims=True)
        i = jnp.min(jnp.where(s == m, row_id, id_sentinel), axis=0, keepdims=True)
        hit = row_id == i
        s = jnp.where(hit, -jnp.inf, s)
        if r < n_rank:
            rank = jnp.where(hit, float(r), rank)
        vals.append(m)
        ids.append(i)
    return vals, ids, rank


def _extract_values(s, n_take):
    vals = []
    for _ in range(n_take):
        m = jnp.max(s, axis=0, keepdims=True)
        s = jnp.where(s == m, -jnp.inf, s)
        vals.append(m)
    removed = jnp.sum(jnp.where(s == -jnp.inf, 1.0, 0.0), axis=0, keepdims=True)
    return vals, jnp.where(removed == float(n_take), 0.0, 1.0)


def _route_kernel(x_ref, wqt_ref, keys_ref, cid_ref,
                  xt_ref, sa_ref, sb_ref, e1_ref, e2_ref, th_ref, ra_ref, rb_ref, tau_ref, ct_ref, flag_ref,
                  qt_ref, top_ref, cand_ref, *, tile, segs):
    xb = x_ref[...].astype(BF16)
    xt_ref[...] = xb.T
    qt_ref[...] = _dot_nt(wqt_ref[...], xb).astype(BF16)
    key_id = lax.broadcasted_iota(jnp.int32, (PEER_N_KEYS, LANES), 0).astype(F32)
    cand_id = cid_ref[...]

    def head(h, carry):
        rows = pl.ds(pl.multiple_of(h * PEER_N_KEYS, PEER_N_KEYS), PEER_N_KEYS)
        slab = pl.ds(pl.multiple_of(h * SUBLANES, SUBLANES), SUBLANES)

        def select(g, s0, s1, ranks):
            top0, top1, cand = top_ref.at[g, 0], top_ref.at[g, 1], cand_ref.at[g]
            off = 0
            for a, nb in segs:
                cand[off:off + nb, :] = top0[a:a + 1, :] + top1[0:nb, :]
                off += nb
            cand[off:off + 8, :] = top0[8:16, :] + top1[0:1, :]
            off += 8
            cand[off:off + 1, :] = top0[16:17, :] + top1[0:1, :]
            cand[off + 1:off + 2, :] = top0[0:1, :] + top1[16:17, :]
            cand[off + 2:, :] = jnp.full((N_CAND - off - 2, LANES), -jnp.inf, F32)
            sums, cids, _ = _extract_top(cand[...], PEER_TOPK + 1, 0, cand_id, 4000.0)
            z = jnp.zeros_like(sums[0])
            for r in range(PEER_TOPK):
                z = z + jnp.exp(sums[r] - sums[0])
            tau = sums[PEER_TOPK - 1]
            e1_ref[g, rows, :] = jnp.exp(s0 - top0[0:1, :]) / z
            e2_ref[g, rows, :] = jnp.exp(s1 - top1[0:1, :])
            vb = top1[0:PEER_TOPK, :]
            theta = jnp.full((PEER_N_KEYS, LANES), jnp.inf, F32)
            for a in range(PEER_TOPK):
                va = top0[a:a + 1, :]
                th_a = jnp.min(jnp.where(va + vb >= tau, vb, jnp.inf), axis=0, keepdims=True)
                match = (s0 == va) if ranks is None else (ranks[0] == float(a))
                theta = jnp.where(match, th_a, theta)
            th_ref[g, rows, :] = theta
            tied = jnp.where(sums[PEER_TOPK] == tau, 1.0, 0.0)
            tau_ref[g, slab, :] = jnp.broadcast_to(tau, (SUBLANES, LANES))
            ct_ref[g, slab, :] = jnp.broadcast_to(cids[PEER_TOPK - 1], (SUBLANES, LANES))
            flag_ref[g, slab, :] = jnp.broadcast_to(tied, (SUBLANES, LANES))
            return tied

        redo = []
        for g in range(tile // LANES):
            lanes = slice(g * LANES, (g + 1) * LANES)
            scores = []
            repeated = jnp.zeros((1, LANES), F32)
            for c in range(2):
                hc = h * 2 + c
                q = qt_ref[pl.ds(pl.multiple_of(hc * PEER_HALF, PEER_HALF), PEER_HALF), lanes]
                s = _dot(keys_ref[hc], q)
                vals, rep = _extract_values(s, PEER_TOPK + 1)
                for r, v in enumerate(vals):
                    top_ref[g, c, r:r + 1, :] = v
                repeated = jnp.maximum(repeated, rep)
                scores.append(s)
            s0, s1 = scores
            sa_ref[g, rows, :] = s0
            sb_ref[g, rows, :] = s1
            ra_ref[g, rows, :] = jnp.full((PEER_N_KEYS, LANES), NOT_RANKED, F32)
            rb_ref[g, rows, :] = jnp.full((PEER_N_KEYS, LANES), NOT_RANKED, rb_ref.dtype)
            tied = select(g, s0, s1, None)
            redo.append(jnp.max(jnp.maximum(repeated, tied)) > 0.0)

        def exact_group(g):
            ranks = []
            for c, sc in enumerate((sa_ref[g, rows, :], sb_ref[g, rows, :])):
                vals, _, rank = _extract_top(sc, PEER_TOPK + 1, PEER_TOPK, key_id, float(PEER_N_KEYS))
                for r, v in enumerate(vals):
                    top_ref[g, c, r:r + 1, :] = v
                ranks.append(rank)
            ra_ref[g, rows, :] = ranks[0] * float(PEER_TOPK)
            rb_ref[g, rows, :] = ranks[1].astype(rb_ref.dtype)
            select(g, sa_ref[g, rows, :], sb_ref[g, rows, :], ranks)

        for g in range(tile // LANES):
            pl.when(redo[g])(functools.partial(exact_group, g))
        return carry

    lax.fori_loop(0, PEER_HEADS, head, 0)


def _peer_route(x2d, wqt, keys, cand_ids):
    n = x2d.shape[0]
    tile = min(ROUTE_TILE, n)
    segs, _ = _candidate_tables()
    hk = PEER_HEADS * PEER_N_KEYS
    full = lambda a: pl.BlockSpec(a.shape, lambda i: (0,) * a.ndim)
    groups = tile // LANES
    slab = lambda r: pl.BlockSpec((groups, r, LANES), lambda i: (i, 0, 0))
    slab_shape = lambda r: (n // LANES, r, LANES)
    hs = PEER_HEADS * SUBLANES
    outs = ([((D_MODEL, n), BF16)] + [(slab_shape(hk), F32)] * 6 + [(slab_shape(hk), BF16)]
            + [(slab_shape(hs), F32)] * 3)
    out_specs = ([pl.BlockSpec((D_MODEL, tile), lambda i: (0, i))] + [slab(hk)] * 7 + [slab(hs)] * 3)
    return pl.pallas_call(
        functools.partial(_route_kernel, tile=tile, segs=segs),
        grid=(n // tile,),
        in_specs=[pl.BlockSpec((tile, D_MODEL), lambda i: (i, 0)), full(wqt), full(keys), full(cand_ids)],
        out_specs=out_specs,
        out_shape=[jax.ShapeDtypeStruct(s, d) for s, d in outs],
        scratch_shapes=[pltpu.VMEM((2 * PEER_HEADS * PEER_HALF, tile), BF16),
                        pltpu.VMEM((groups, 2, 24, LANES), F32),
                        pltpu.VMEM((groups, N_CAND, LANES), F32)],
        compiler_params=_params(("parallel",)),
        name="peer_route",
    )(x2d, wqt, keys, cand_ids)


def _gelu(h):
    return 0.5 * h * (1.0 + lax.erf(h * (1.0 / math.sqrt(2.0))))


def _dense_kernel(xt_ref, u_ref, vt_ref, sa_ref, sb_ref, e1_ref, e2_ref, th_ref, ra_ref, rb_ref, tau_ref, ct_ref,
                  flag_ref, x_ref, g_ref, b_ref, y_ref, acc_ref, h_ref, w_ref, tied_ref, *, tile, etile):
    s = pl.program_id(1)
    n_tiles = pl.num_programs(1) - 1

    @pl.when(s == 0)
    def _():
        acc_ref[...] = jnp.zeros_like(acc_ref)
        tied_ref[0] = (jnp.max(flag_ref[...]) > 0.0).astype(jnp.int32)
        h_ref[0] = _dot(u_ref[...], xt_ref[...])

    tied = tied_ref[0] > 0

    def step(parity, exact):
        cur, nxt = 1 - parity, parity
        h_ref[nxt] = _dot(u_ref[...], xt_ref[...])
        down = None
        for r in range(etile // PEER_N_KEYS):
            i0 = (s - 1) * (etile // PEER_N_KEYS) + r
            for g in range(tile // LANES):
                lanes = slice(g * LANES, (g + 1) * LANES)
                row_of = lambda ref, h: jnp.broadcast_to(ref[g, pl.ds(h * PEER_N_KEYS + i0, 1), :], (SUBLANES, LANES))
                head_of = lambda ref, h: ref[g, h * SUBLANES:(h + 1) * SUBLANES, :]
                e1_b = [row_of(e1_ref, h) for h in range(PEER_HEADS)]
                if exact:
                    a_b = [row_of(sa_ref, h) for h in range(PEER_HEADS)]
                    ra_b = [row_of(ra_ref, h) for h in range(PEER_HEADS)]
                    tau_b = [head_of(tau_ref, h) for h in range(PEER_HEADS)]
                    ct_b = [head_of(ct_ref, h) for h in range(PEER_HEADS)]
                else:
                    th_b = [row_of(th_ref, h) for h in range(PEER_HEADS)]
                for jc in range(PEER_N_KEYS // GATE_ROWS):
                    if exact:
                        rb = [rb_ref[g, h * PEER_N_KEYS + jc * GATE_ROWS:h * PEER_N_KEYS + (jc + 1) * GATE_ROWS, :]
                              .astype(F32) for h in range(PEER_HEADS)]
                    parts = []
                    for sub in range(GATE_ROWS // SUBLANES):
                        j0 = jc * GATE_ROWS + sub * SUBLANES
                        gate = None
                        for h in range(PEER_HEADS):
                            rows = slice(h * PEER_N_KEYS + j0, h * PEER_N_KEYS + j0 + SUBLANES)
                            if exact:
                                total = a_b[h] + sb_ref[g, rows, :]
                                cand = ra_b[h] + rb[h][sub * SUBLANES:(sub + 1) * SUBLANES]
                                keep = (total > tau_b[h]) | ((total == tau_b[h]) & (cand <= ct_b[h]))
                            else:
                                keep = sb_ref[g, rows, :] >= th_b[h]
                            term = jnp.where(keep, e1_b[h] * e2_ref[g, rows, :], 0.0)
                            gate = term if gate is None else gate + term
                        hv = h_ref[cur, r * PEER_N_KEYS + j0:r * PEER_N_KEYS + j0 + SUBLANES, lanes]
                        parts.append(gate * _gelu(hv))
                    rows = slice(r * PEER_N_KEYS + jc * GATE_ROWS, r * PEER_N_KEYS + (jc + 1) * GATE_ROWS)
                    w_ref[rows, lanes] = jnp.concatenate(parts, axis=0).astype(BF16)
            block = slice(r * PEER_N_KEYS, (r + 1) * PEER_N_KEYS)
            part = _dot(vt_ref[:, block], w_ref[block, :])
            down = part if down is None else down + part
        acc_ref[...] += down

    for parity in range(2):
        for exact in (False, True):
            cond = jnp.logical_and(jnp.logical_and(s >= 1, lax.rem(s, 2) == parity), tied == exact)
            pl.when(cond)(functools.partial(step, parity, exact))

    @pl.when(s == n_tiles)
    def _():
        y = DEEPNORM_ALPHA * x_ref[...] + acc_ref[...].T
        y_ref[...] = _layer_norm(y, g_ref[...], b_ref[...])


def _peer_dense(xt, u, vt, route, x2d, g, b):
    n = x2d.shape[0]
    tile = min(DENSE_TOKEN_TILE, n)
    etile = DENSE_EXPERT_TILE
    steps = PEER_N_EXPERTS // etile
    hk = PEER_HEADS * PEER_N_KEYS
    slab = lambda r: pl.BlockSpec((tile // LANES, r, LANES), lambda i, e: (i, 0, 0))
    full = lambda a: pl.BlockSpec(a.shape, lambda i, e: (0,) * a.ndim)
    return pl.pallas_call(
        functools.partial(_dense_kernel, tile=tile, etile=etile),
        grid=(n // tile, steps + 1),
        in_specs=[pl.BlockSpec((D_MODEL, tile), lambda i, e: (0, i)),
                  pl.BlockSpec((etile, D_MODEL), lambda i, e: (jnp.minimum(e, steps - 1), 0)),
                  pl.BlockSpec((D_MODEL, etile), lambda i, e: (0, jnp.maximum(e - 1, 0)))]
                 + [slab(hk)] * 7 + [slab(PEER_HEADS * SUBLANES)] * 3
                 + [pl.BlockSpec((tile, D_MODEL), lambda i, e: (i, 0)), full(g), full(b)],
        out_specs=pl.BlockSpec((tile, D_MODEL), lambda i, e: (i, 0)),
        out_shape=jax.ShapeDtypeStruct((n, D_MODEL), F32),
        scratch_shapes=[pltpu.VMEM((D_MODEL, tile), F32), pltpu.VMEM((2, etile, tile), F32),
                        pltpu.VMEM((etile, tile), BF16), pltpu.SMEM((1,), jnp.int32)],
        compiler_params=_params(("parallel", "arbitrary")),
        name="peer_dense",
    )(xt, u, vt, *route, x2d, g, b)


def _rope_tables(seq_len):
    pos = jnp.arange(seq_len, dtype=F32)
    inv_freq = ROPE_THETA ** (-jnp.arange(0, MLA_ROPE_DIM, 2, dtype=F32) / MLA_ROPE_DIM)
    ang = pos[:, None] * inv_freq[None, :]
    cos, sin = jnp.cos(ang), jnp.sin(ang)
    ones = jnp.ones((seq_len, MLA_NOPE_DIM), F32)
    pad = jnp.zeros((seq_len, LANES - MLA_QK_DIM), F32)
    cos_tab = jnp.concatenate([ones, cos, cos, pad], axis=1)
    sin_tab = jnp.concatenate([0.0 * ones, -sin, sin, pad], axis=1)
    return cos_tab, sin_tab


def _rotate_half_columns(w):
    half = MLA_ROPE_DIM // 2
    return jnp.concatenate([w[..., half:], w[..., :half]], axis=-1)


def _mixer_weights(w_in, w_uq, w_ukv):
    sb_scale = 1.0 / math.sqrt(SB_HEAD_DIM)
    wsb = jnp.concatenate([w_in[:, :D_SB] * sb_scale, w_in[:, D_SB:3 * D_SB]], axis=1).astype(BF16)
    wc = w_in[:, 3 * D_SB:3 * D_SB + MLA_Q_RANK + MLA_KV_RANK].astype(BF16)
    w_kr = w_in[:, 3 * D_SB + MLA_Q_RANK + MLA_KV_RANK:]
    lead = jnp.zeros((D_MODEL, MLA_NOPE_DIM), F32)
    trail = jnp.zeros((D_MODEL, LANES - MLA_QK_DIM), F32)
    wkr = jnp.concatenate([lead, w_kr, trail, lead, _rotate_half_columns(w_kr), trail], axis=1).astype(BF16)

    wq = w_uq.reshape(MLA_Q_RANK, MLA_HEADS, MLA_QK_DIM)
    qpad = jnp.zeros((MLA_Q_RANK, MLA_HEADS, LANES - MLA_QK_DIM), F32)
    q_plain = jnp.concatenate([wq, qpad], axis=2)
    q_rot = jnp.concatenate([0.0 * wq[..., :MLA_NOPE_DIM], _rotate_half_columns(wq[..., MLA_NOPE_DIM:]), qpad], axis=2)
    wq2 = jnp.concatenate([q_plain.reshape(MLA_Q_RANK, -1), q_rot.reshape(MLA_Q_RANK, -1)], axis=1).astype(BF16)

    wkv = w_ukv.reshape(MLA_KV_RANK, MLA_HEADS, MLA_NOPE_DIM + MLA_V_DIM)
    kpad = jnp.zeros((MLA_KV_RANK, MLA_HEADS, LANES - MLA_NOPE_DIM), F32)
    wk = jnp.concatenate([wkv[..., :MLA_NOPE_DIM], kpad], axis=2).reshape(MLA_KV_RANK, -1)
    wv = wkv[..., MLA_NOPE_DIM:].reshape(MLA_KV_RANK, -1)
    wkv2 = jnp.concatenate([wk, wv], axis=1).astype(BF16)
    return wsb, wc, wkr, wq2, wkv2


def kernel(x, w_in, g_cq, w_uq, g_ckv, w_ukv, w_o, ln1_g, ln1_b, peer_wq, peer_keys, peer_u, peer_v, ln2_g, ln2_b):
    b, s, d = x.shape
    n = b * s
    cos_tab, sin_tab = _rope_tables(s)
    tile = min(ATTN_TILE, s)
    strict_upper = (jnp.arange(tile)[:, None] > jnp.arange(tile)[None, :]).astype(BF16)
    _, cand_ids = _candidate_tables()
    cand_ids = jnp.asarray(np.broadcast_to(cand_ids[:, None], (N_CAND, LANES)).copy())

    x2d = x.reshape(n, d)
    for l in range(DEPTH):
        wsb, wc, wkr, wq2, wkv2 = _mixer_weights(w_in[l], w_uq[l], w_ukv[l])
        sbq, sbk, sbv, mq, mk, mv = _in_proj(x2d, s, wsb, wc, wkr, g_cq[l][None, :], g_ckv[l][None, :], wq2, wkv2,
                                             cos_tab, sin_tab)
        sh = lambda a: a.reshape(b, s, a.shape[-1])
        o_sb = _sb_attention(sh(sbq), sh(sbk), sh(sbv), strict_upper)
        o_mla = _mla_attention(sh(mq), sh(mk), sh(mv))
        x2d = _out_proj_ln(o_sb.reshape(n, D_SB), o_mla.reshape(n, D_MLA_OUT), x2d, w_o[l].astype(BF16),
                           ln1_g[l][None, :], ln1_b[l][None, :])
        wqt = peer_wq[l].T.astype(BF16)
        keys = peer_keys[l].reshape(PEER_HEADS * 2, PEER_N_KEYS, PEER_HALF).astype(BF16)
        route = _peer_route(x2d, wqt, keys, cand_ids)
        x2d = _peer_dense(route[0], peer_u[l].astype(BF16), peer_v[l].T.astype(BF16), route[1:], x2d,
                          ln2_g[l][None, :], ln2_b[l][None, :])
    return x2d.reshape(b, s, d)
```

```python
import functools
import math

import jax
import jax.numpy as jnp
import numpy as np
from jax import lax
from jax.experimental import pallas as pl
from jax.experimental.pallas import tpu as pltpu

D_MODEL = 1024
DEPTH = 2
SB_HEADS = 8
SB_HEAD_DIM = 64
D_SB = SB_HEADS * SB_HEAD_DIM
MLA_HEADS = 8
MLA_NOPE_DIM = 64
MLA_ROPE_DIM = 32
MLA_QK_DIM = MLA_NOPE_DIM + MLA_ROPE_DIM
MLA_V_DIM = 64
MLA_Q_RANK = 256
MLA_KV_RANK = 128
D_MLA_OUT = MLA_HEADS * MLA_V_DIM
ROPE_THETA = 10000.0
PEER_HEADS = 8
PEER_N_KEYS = 128
PEER_N_EXPERTS = PEER_N_KEYS * PEER_N_KEYS
PEER_HALF = 128
PEER_TOPK = 16
DEEPNORM_ALPHA = (2.0 * DEPTH) ** 0.25
LN_EPS = 1e-5
RMS_EPS = 1e-6

LANES = 128
SUBLANES = 8
VMEM_LIMIT_BYTES = 56 * 1024 * 1024

TOKEN_TILE = 512
ATTN_TILE = 256
MLA_TILE = 1024
ROUTE_TILE = 512
DENSE_TOKEN_TILE = 256
DENSE_EXPERT_TILE = 512
GATE_ROWS = 16

EXP_UNDERFLOW = -120.0
NOT_RANKED = 4096.0
N_CAND = 56

F32 = jnp.float32
BF16 = jnp.bfloat16


def _dot(a, b):
    return jnp.dot(a, b, preferred_element_type=F32)


def _dot_nt(a, b):
    return lax.dot_general(a, b, (((1,), (1,)), ((), ())), preferred_element_type=F32)


def _params(semantics, flags=None):
    return pltpu.CompilerParams(dimension_semantics=semantics, vmem_limit_bytes=VMEM_LIMIT_BYTES, flags=flags)


def _in_proj_kernel(x_ref, wsb_ref, wc_ref, wkr_ref, gq_ref, gkv_ref, wq2_ref, wkv_ref, cos_ref, sin_ref,
                    sbq_ref, sbk_ref, sbv_ref, mq_ref, mk_ref, mv_ref):
    xb = x_ref[...].astype(BF16)
    sb = _dot(xb, wsb_ref[...])
    sbq_ref[...] = sb[:, :D_SB].astype(BF16)
    sbk_ref[...] = sb[:, D_SB:2 * D_SB].astype(BF16)
    sbv_ref[...] = sb[:, 2 * D_SB:].astype(BF16)

    c = _dot(xb, wc_ref[...])
    cq = c[:, :MLA_Q_RANK]
    ckv = c[:, MLA_Q_RANK:]
    nq = cq * lax.rsqrt(jnp.mean(cq * cq, axis=-1, keepdims=True) + RMS_EPS) * gq_ref[...]
    nkv = ckv * lax.rsqrt(jnp.mean(ckv * ckv, axis=-1, keepdims=True) + RMS_EPS) * gkv_ref[...]

    cos = cos_ref[...]
    sin = sin_ref[...]
    kr = _dot(xb, wkr_ref[...])
    k_rope = kr[:, :LANES] * cos + kr[:, LANES:] * sin

    q2 = _dot(nq.astype(BF16), wq2_ref[...])
    kv = _dot(nkv.astype(BF16), wkv_ref[...])
    scale = 1.0 / math.sqrt(MLA_QK_DIM)
    for h in range(MLA_HEADS):
        lo, hi = h * LANES, (h + 1) * LANES
        qa = q2[:, lo:hi]
        qb = q2[:, MLA_HEADS * LANES + lo:MLA_HEADS * LANES + hi]
        mq_ref[:, lo:hi] = ((qa * cos + qb * sin) * scale).astype(BF16)
        mk_ref[:, lo:hi] = (kv[:, lo:hi] + k_rope).astype(BF16)
    mv_ref[...] = kv[:, MLA_HEADS * LANES:].T.astype(BF16)


def _in_proj(x2d, seq_len, wsb, wc, wkr, gq, gkv, wq2, wkv, cos_tab, sin_tab):
    n = x2d.shape[0]
    tm = min(TOKEN_TILE, seq_len)
    pos_tiles = seq_len // tm
    full = lambda a: pl.BlockSpec(a.shape, lambda i: (0,) * a.ndim)
    row = lambda w: pl.BlockSpec((tm, w), lambda i: (i, 0))
    col = lambda r: pl.BlockSpec((r, tm), lambda i: (0, i))
    pos = pl.BlockSpec((tm, LANES), lambda i: (i % pos_tiles, 0))
    wide = MLA_HEADS * LANES
    outs = [(n, D_SB), (n, D_SB), (n, D_SB), (n, wide), (n, wide), (D_MLA_OUT, n)]
    out_specs = [row(D_SB), row(D_SB), row(D_SB), row(wide), row(wide), col(D_MLA_OUT)]
    return pl.pallas_call(
        _in_proj_kernel,
        grid=(n // tm,),
        in_specs=[row(D_MODEL), full(wsb), full(wc), full(wkr), full(gq), full(gkv), full(wq2), full(wkv), pos, pos],
        out_specs=out_specs,
        out_shape=[jax.ShapeDtypeStruct(s, BF16) for s in outs],
        compiler_params=_params(("parallel",)),
        name="in_proj",
    )(x2d, wsb, wc, wkr, gq, gkv, wq2, wkv, cos_tab, sin_tab)


def _sb_attn_kernel(q_ref, k_ref, v_ref, u_ref, o_ref, acc_ref, c_ref, *, tile):
    qi = pl.program_id(2)
    q2 = q_ref[0]
    lane = lax.broadcasted_iota(jnp.int32, (tile, LANES), 1)
    zero = jnp.zeros_like(q2)
    q_heads = (jnp.where(lane < SB_HEAD_DIM, q2, zero), jnp.where(lane >= SB_HEAD_DIM, q2, zero))
    u = u_ref[...]
    r_id = lax.broadcasted_iota(jnp.int32, (tile, tile), 0)
    c_id = lax.broadcasted_iota(jnp.int32, (tile, tile), 1)
    causal = c_id < r_id

    acc_ref[...] = jnp.zeros_like(acc_ref)
    c_ref[...] = jnp.zeros_like(c_ref)

    def block(j, masked):
        start = pl.multiple_of(j * tile, tile)
        kb = k_ref[0, pl.ds(start, tile), :]
        vb = v_ref[0, pl.ds(start, tile), :]
        for hh in range(2):
            z = _dot_nt(q_heads[hh], kb)
            softplus = jnp.maximum(z, 0.0) + jnp.log1p(jnp.exp(-jnp.abs(z)))
            log_beta = z - softplus
            log_1m = -softplus
            if masked:
                log_1m = jnp.where(causal, log_1m, 0.0)
            hi = log_1m.astype(BF16)
            lo = (log_1m - hi.astype(F32)).astype(BF16)
            tail = _dot(hi, u) + _dot(lo, u) + c_ref[hh]
            w = jnp.exp(log_beta + tail)
            if masked:
                w = jnp.where(causal, w, 0.0)
            acc_ref[hh] += _dot(w.astype(BF16), vb)
            c_ref[hh] += jnp.sum(log_1m, axis=1, keepdims=True)

    block(qi, True)

    def more(carry):
        t, c_max = carry
        return jnp.logical_and(t < qi, c_max > EXP_UNDERFLOW)

    def body(carry):
        t, _ = carry
        block(qi - 1 - t, False)
        return t + 1, jnp.max(c_ref[...])

    lax.while_loop(more, body, (jnp.int32(0), jnp.max(c_ref[...])))
    o_ref[0] = jnp.where(lane < SB_HEAD_DIM, acc_ref[0], acc_ref[1]).astype(o_ref.dtype)


def _sb_attention(q, k, v, u):
    b, s, _ = q.shape
    tile = min(ATTN_TILE, s)
    pairs = D_SB // LANES
    q_spec = pl.BlockSpec((1, tile, LANES), lambda bi, p, i: (bi, i, p))
    kv_spec = pl.BlockSpec((1, s, LANES), lambda bi, p, i: (bi, 0, p))
    return pl.pallas_call(
        functools.partial(_sb_attn_kernel, tile=tile),
        grid=(b, pairs, s // tile),
        in_specs=[q_spec, kv_spec, kv_spec, pl.BlockSpec((tile, tile), lambda bi, p, i: (0, 0))],
        out_specs=q_spec,
        out_shape=jax.ShapeDtypeStruct((b, s, D_SB), BF16),
        scratch_shapes=[pltpu.VMEM((2, tile, LANES), F32), pltpu.VMEM((2, tile, 1), F32)],
        compiler_params=_params(("parallel", "parallel", "arbitrary")),
        name="sb_attn",
    )(q, k, v, u)


def _mla_attn_kernel(q_ref, k_ref, vt_ref, o_ref, m_ref, l_ref, acc_ref, *, tile):
    qi = pl.program_id(2)
    q2 = q_ref[0]
    k_id = lax.broadcasted_iota(jnp.int32, (tile, tile), 0)
    q_id = lax.broadcasted_iota(jnp.int32, (tile, tile), 1)
    causal = k_id <= q_id

    m_ref[...] = jnp.full_like(m_ref, -jnp.inf)
    l_ref[...] = jnp.zeros_like(l_ref)
    acc_ref[...] = jnp.zeros_like(acc_ref)

    def block(j, masked):
        start = pl.multiple_of(j * tile, tile)
        kb = k_ref[0, pl.ds(start, tile), :]
        vtb = vt_ref[:, pl.ds(start, tile)]
        for hh in range(2):
            s = _dot_nt(kb[:, hh * LANES:(hh + 1) * LANES], q2[:, hh * LANES:(hh + 1) * LANES])
            if masked:
                s = jnp.where(causal, s, -jnp.inf)
            m_old = m_ref[hh]
            m_new = jnp.maximum(m_old, jnp.max(s, axis=0, keepdims=True))
            p = jnp.exp(s - m_new)
            alpha = jnp.exp(m_old - m_new)
            l_ref[hh] = alpha * l_ref[hh] + jnp.sum(p, axis=0, keepdims=True)
            pv = _dot(vtb[hh * MLA_V_DIM:(hh + 1) * MLA_V_DIM, :], p.astype(BF16))
            acc_ref[hh] = alpha * acc_ref[hh] + pv
            m_ref[hh] = m_new

    block(qi, True)

    def body(j, carry):
        block(j, False)
        return carry

    lax.fori_loop(0, qi, body, 0)
    o_t = jnp.concatenate([acc_ref[0] / l_ref[0], acc_ref[1] / l_ref[1]], axis=0)
    o_ref[0] = o_t.T.astype(o_ref.dtype)


def _mla_attention(q, k, vt):
    b, s, _ = q.shape
    tile = min(MLA_TILE, s)
    pairs = MLA_HEADS // 2
    q_spec = pl.BlockSpec((1, tile, 2 * LANES), lambda bi, p, i: (bi, i, p))
    k_spec = pl.BlockSpec((1, s, 2 * LANES), lambda bi, p, i: (bi, 0, p))
    vt_spec = pl.BlockSpec((LANES, s), lambda bi, p, i: (p, bi))
    o_spec = pl.BlockSpec((1, tile, LANES), lambda bi, p, i: (bi, i, p))
    return pl.pallas_call(
        functools.partial(_mla_attn_kernel, tile=tile),
        grid=(b, pairs, s // tile),
        in_specs=[q_spec, k_spec, vt_spec],
        out_specs=o_spec,
        out_shape=jax.ShapeDtypeStruct((b, s, D_MLA_OUT), BF16),
        scratch_shapes=[pltpu.VMEM((2, 1, tile), F32), pltpu.VMEM((2, 1, tile), F32),
                        pltpu.VMEM((2, MLA_V_DIM, tile), F32)],
        compiler_params=_params(("parallel", "parallel", "arbitrary")),
        name="mla_attn",
    )(q, k, vt)


def _layer_norm(y, g, b):
    mu = jnp.mean(y, axis=-1, keepdims=True)
    d = y - mu
    var = jnp.mean(d * d, axis=-1, keepdims=True)
    return d * lax.rsqrt(var + LN_EPS) * g + b


def _out_proj_kernel(osb_ref, omla_ref, x_ref, wo_ref, g_ref, b_ref, y_ref):
    mix = _dot(osb_ref[...], wo_ref[:D_SB, :]) + _dot(omla_ref[...], wo_ref[D_SB:, :])
    y_ref[...] = _layer_norm(DEEPNORM_ALPHA * x_ref[...] + mix, g_ref[...], b_ref[...])


def _out_proj_ln(osb, omla, x2d, wo, g, b):
    n = x2d.shape[0]
    tm = min(TOKEN_TILE, n)
    full = lambda a: pl.BlockSpec(a.shape, lambda i: (0,) * a.ndim)
    row = lambda w: pl.BlockSpec((tm, w), lambda i: (i, 0))
    return pl.pallas_call(
        _out_proj_kernel,
        grid=(n // tm,),
        in_specs=[row(D_SB), row(D_MLA_OUT), row(D_MODEL), full(wo), full(g), full(b)],
        out_specs=row(D_MODEL),
        out_shape=jax.ShapeDtypeStruct((n, D_MODEL), F32),
        compiler_params=_params(("parallel",)),
        name="out_proj_ln",
    )(osb, omla, x2d, wo, g, b)


def _candidate_tables():
    segs = [(a, PEER_TOPK // (a + 1)) for a in range(8)]
    ids = []
    for a, nb in segs:
        ids += [a * PEER_TOPK + b for b in range(nb)]
    ids += [a * PEER_TOPK for a in range(8, PEER_TOPK)]
    ids += [1000.0, 1001.0]
    ids += [2000.0 + i for i in range(N_CAND - len(ids))]
    return segs, np.asarray(ids, np.float32)


def _extract_top(s, n_take, n_rank, row_id, id_sentinel):
    rank = jnp.full(s.shape, NOT_RANKED, F32)
    vals, ids = [], []
    for r in range(n_take):
        m = jnp.max(s, axis=0, keepdims=True)
        i = jnp.min(jnp.where(s == m, row_id, id_sentinel), axis=0, keepdims=True)
        hit = row_id == i
        s = jnp.where(hit, -jnp.inf, s)
        if r < n_rank:
            rank = jnp.where(hit, float(r), rank)
        vals.append(m)
        ids.append(i)
    return vals, ids, rank


def _extract_values(s, n_take):
    vals = []
    for _ in range(n_take):
        m = jnp.max(s, axis=0, keepdims=True)
        s = jnp.where(s == m, -jnp.inf, s)
        vals.append(m)
    removed = jnp.sum(jnp.where(s == -jnp.inf, 1.0, 0.0), axis=0, keepdims=True)
    return vals, jnp.where(removed == float(n_take), 0.0, 1.0)


def _route_kernel(x_ref, wqt_ref, keys_ref, cid_ref,
                  xt_ref, sa_ref, sb_ref, e1_ref, e2_ref, th_ref, ra_ref, rb_ref, tau_ref, ct_ref, flag_ref,
                  qt_ref, top_ref, cand_ref, *, tile, segs):
    xb = x_ref[...].astype(BF16)
    xt_ref[...] = xb.T
    qt_ref[...] = _dot_nt(wqt_ref[...], xb).astype(BF16)
    key_id = lax.broadcasted_iota(jnp.int32, (PEER_N_KEYS, LANES), 0).astype(F32)
    cand_id = cid_ref[...]

    def head(h, carry):
        rows = pl.ds(pl.multiple_of(h * PEER_N_KEYS, PEER_N_KEYS), PEER_N_KEYS)
        slab = pl.ds(pl.multiple_of(h * SUBLANES, SUBLANES), SUBLANES)

        def select(g, s0, s1, ranks):
            top0, top1, cand = top_ref.at[g, 0], top_ref.at[g, 1], cand_ref.at[g]
            off = 0
            for a, nb in segs:
                cand[off:off + nb, :] = top0[a:a + 1, :] + top1[0:nb, :]
                off += nb
            cand[off:off + 8, :] = top0[8:16, :] + top1[0:1, :]
            off += 8
            cand[off:off + 1, :] = top0[16:17, :] + top1[0:1, :]
            cand[off + 1:off + 2, :] = top0[0:1, :] + top1[16:17, :]
            cand[off + 2:, :] = jnp.full((N_CAND - off - 2, LANES), -jnp.inf, F32)
            sums, cids, _ = _extract_top(cand[...], PEER_TOPK + 1, 0, cand_id, 4000.0)
            z = jnp.zeros_like(sums[0])
            for r in range(PEER_TOPK):
                z = z + jnp.exp(sums[r] - sums[0])
            tau = sums[PEER_TOPK - 1]
            e1_ref[g, rows, :] = jnp.exp(s0 - top0[0:1, :]) / z
            e2_ref[g, rows, :] = jnp.exp(s1 - top1[0:1, :])
            vb = top1[0:PEER_TOPK, :]
            theta = jnp.full((PEER_N_KEYS, LANES), jnp.inf, F32)
            for a in range(PEER_TOPK):
                va = top0[a:a + 1, :]
                th_a = jnp.min(jnp.where(va + vb >= tau, vb, jnp.inf), axis=0, keepdims=True)
                match = (s0 == va) if ranks is None else (ranks[0] == float(a))
                theta = jnp.where(match, th_a, theta)
            th_ref[g, rows, :] = theta
            tied = jnp.where(sums[PEER_TOPK] == tau, 1.0, 0.0)
            tau_ref[g, slab, :] = jnp.broadcast_to(tau, (SUBLANES, LANES))
            ct_ref[g, slab, :] = jnp.broadcast_to(cids[PEER_TOPK - 1], (SUBLANES, LANES))
            flag_ref[g, slab, :] = jnp.broadcast_to(tied, (SUBLANES, LANES))
            return tied

        redo = []
        for g in range(tile // LANES):
            lanes = slice(g * LANES, (g + 1) * LANES)
            scores = []
            repeated = jnp.zeros((1, LANES), F32)
            for c in range(2):
                hc = h * 2 + c
                q = qt_ref[pl.ds(pl.multiple_of(hc * PEER_HALF, PEER_HALF), PEER_HALF), lanes]
                s = _dot(keys_ref[hc], q)
                vals, rep = _extract_values(s, PEER_TOPK + 1)
                for r, v in enumerate(vals):
                    top_ref[g, c, r:r + 1, :] = v
                repeated = jnp.maximum(repeated, rep)
                scores.append(s)
            s0, s1 = scores
            sa_ref[g, rows, :] = s0
            sb_ref[g, rows, :] = s1
            ra_ref[g, rows, :] = jnp.full((PEER_N_KEYS, LANES), NOT_RANKED, F32)
            rb_ref[g, rows, :] = jnp.full((PEER_N_KEYS, LANES), NOT_RANKED, rb_ref.dtype)
            tied = select(g, s0, s1, None)
            redo.append(jnp.max(jnp.maximum(repeated, tied)) > 0.0)

        def exact_group(g):
            ranks = []
            for c, sc in enumerate((sa_ref[g, rows, :], sb_ref[g, rows, :])):
                vals, _, rank = _extract_top(sc, PEER_TOPK + 1, PEER_TOPK, key_id, float(PEER_N_KEYS))
                for r, v in enumerate(vals):
                    top_ref[g, c, r:r + 1, :] = v
                ranks.append(rank)
            ra_ref[g, rows, :] = ranks[0] * float(PEER_TOPK)
            rb_ref[g, rows, :] = ranks[1].astype(rb_ref.dtype)
            select(g, sa_ref[g, rows, :], sb_ref[g, rows, :], ranks)

        for g in range(tile // LANES):
            pl.when(redo[g])(functools.partial(exact_group, g))
        return carry

    lax.fori_loop(0, PEER_HEADS, head, 0)


def _peer_route(x2d, wqt, keys, cand_ids):
    n = x2d.shape[0]
    tile = min(ROUTE_TILE, n)
    segs, _ = _candidate_tables()
    hk = PEER_HEADS * PEER_N_KEYS
    full = lambda a: pl.BlockSpec(a.shape, lambda i: (0,) * a.ndim)
    groups = tile // LANES
    slab = lambda r: pl.BlockSpec((groups, r, LANES), lambda i: (i, 0, 0))
    slab_shape = lambda r: (n // LANES, r, LANES)
    hs = PEER_HEADS * SUBLANES
    outs = ([((D_MODEL, n), BF16)] + [(slab_shape(hk), F32)] * 6 + [(slab_shape(hk), BF16)]
            + [(slab_shape(hs), F32)] * 3)
    out_specs = ([pl.BlockSpec((D_MODEL, tile), lambda i: (0, i))] + [slab(hk)] * 7 + [slab(hs)] * 3)
    return pl.pallas_call(
        functools.partial(_route_kernel, tile=tile, segs=segs),
        grid=(n // tile,),
        in_specs=[pl.BlockSpec((tile, D_MODEL), lambda i: (i, 0)), full(wqt), full(keys), full(cand_ids)],
        out_specs=out_specs,
        out_shape=[jax.ShapeDtypeStruct(s, d) for s, d in outs],
        scratch_shapes=[pltpu.VMEM((2 * PEER_HEADS * PEER_HALF, tile), BF16),
                        pltpu.VMEM((groups, 2, 24, LANES), F32),
                        pltpu.VMEM((groups, N_CAND, LANES), F32)],
        compiler_params=_params(("parallel",)),
        name="peer_route",
    )(x2d, wqt, keys, cand_ids)


def _gelu(h):
    return 0.5 * h * (1.0 + lax.erf(h * (1.0 / math.sqrt(2.0))))


def _dense_kernel(xt_ref, u_ref, vt_ref, sa_ref, sb_ref, e1_ref, e2_ref, th_ref, ra_ref, rb_ref, tau_ref, ct_ref,
                  flag_ref, x_ref, g_ref, b_ref, y_ref, acc_ref, h_ref, w_ref, tied_ref, *, tile, etile):
    s = pl.program_id(1)
    n_tiles = pl.num_programs(1) - 1

    @pl.when(s == 0)
    def _():
        acc_ref[...] = jnp.zeros_like(acc_ref)
        tied_ref[0] = (jnp.max(flag_ref[...]) > 0.0).astype(jnp.int32)
        h_ref[0] = _dot(u_ref[...], xt_ref[...])

    tied = tied_ref[0] > 0

    def step(parity, exact):
        cur, nxt = 1 - parity, parity
        h_ref[nxt] = _dot(u_ref[...], xt_ref[...])
        down = None
        for r in range(etile // PEER_N_KEYS):
            i0 = (s - 1) * (etile // PEER_N_KEYS) + r
            for g in range(tile // LANES):
                lanes = slice(g * LANES, (g + 1) * LANES)
                row_of = lambda ref, h: jnp.broadcast_to(ref[g, pl.ds(h * PEER_N_KEYS + i0, 1), :], (SUBLANES, LANES))
                head_of = lambda ref, h: ref[g, h * SUBLANES:(h + 1) * SUBLANES, :]
                e1_b = [row_of(e1_ref, h) for h in range(PEER_HEADS)]
                if exact:
                    a_b = [row_of(sa_ref, h) for h in range(PEER_HEADS)]
                    ra_b = [row_of(ra_ref, h) for h in range(PEER_HEADS)]
                    tau_b = [head_of(tau_ref, h) for h in range(PEER_HEADS)]
                    ct_b = [head_of(ct_ref, h) for h in range(PEER_HEADS)]
                else:
                    th_b = [row_of(th_ref, h) for h in range(PEER_HEADS)]
                for jc in range(PEER_N_KEYS // GATE_ROWS):
                    if exact:
                        rb = [rb_ref[g, h * PEER_N_KEYS + jc * GATE_ROWS:h * PEER_N_KEYS + (jc + 1) * GATE_ROWS, :]
                              .astype(F32) for h in range(PEER_HEADS)]
                    parts = []
                    for sub in range(GATE_ROWS // SUBLANES):
                        j0 = jc * GATE_ROWS + sub * SUBLANES
                        gate = None
                        for h in range(PEER_HEADS):
                            rows = slice(h * PEER_N_KEYS + j0, h * PEER_N_KEYS + j0 + SUBLANES)
                            if exact:
                                total = a_b[h] + sb_ref[g, rows, :]
                                cand = ra_b[h] + rb[h][sub * SUBLANES:(sub + 1) * SUBLANES]
                                keep = (total > tau_b[h]) | ((total == tau_b[h]) & (cand <= ct_b[h]))
                            else:
                                keep = sb_ref[g, rows, :] >= th_b[h]
                            term = jnp.where(keep, e1_b[h] * e2_ref[g, rows, :], 0.0)
                            gate = term if gate is None else gate + term
                        hv = h_ref[cur, r * PEER_N_KEYS + j0:r * PEER_N_KEYS + j0 + SUBLANES, lanes]
                        parts.append(gate * _gelu(hv))
                    rows = slice(r * PEER_N_KEYS + jc * GATE_ROWS, r * PEER_N_KEYS + (jc + 1) * GATE_ROWS)
                    w_ref[rows, lanes] = jnp.concatenate(parts, axis=0).astype(BF16)
            block = slice(r * PEER_N_KEYS, (r + 1) * PEER_N_KEYS)
            part = _dot(vt_ref[:, block], w_ref[block, :])
            down = part if down is None else down + part
        acc_ref[...] += down

    for parity in range(2):
        for exact in (False, True):
            cond = jnp.logical_and(jnp.logical_and(s >= 1, lax.rem(s, 2) == parity), tied == exact)
            pl.when(cond)(functools.partial(step, parity, exact))

    @pl.when(s == n_tiles)
    def _():
        y = DEEPNORM_ALPHA * x_ref[...] + acc_ref[...].T
        y_ref[...] = _layer_norm(y, g_ref[...], b_ref[...])


def _peer_dense(xt, u, vt, route, x2d, g, b):
    n = x2d.shape[0]
    tile = min(DENSE_TOKEN_TILE, n)
    etile = DENSE_EXPERT_TILE
    steps = PEER_N_EXPERTS // etile
    hk = PEER_HEADS * PEER_N_KEYS
    slab = lambda r: pl.BlockSpec((tile // LANES, r, LANES), lambda i, e: (i, 0, 0))
    full = lambda a: pl.BlockSpec(a.shape, lambda i, e: (0,) * a.ndim)
    return pl.pallas_call(
        functools.partial(_dense_kernel, tile=tile, etile=etile),
        grid=(n // tile, steps + 1),
        in_specs=[pl.BlockSpec((D_MODEL, tile), lambda i, e: (0, i)),
                  pl.BlockSpec((etile, D_MODEL), lambda i, e: (jnp.minimum(e, steps - 1), 0)),
                  pl.BlockSpec((D_MODEL, etile), lambda i, e: (0, jnp.maximum(e - 1, 0)))]
                 + [slab(hk)] * 7 + [slab(PEER_HEADS * SUBLANES)] * 3
                 + [pl.BlockSpec((tile, D_MODEL), lambda i, e: (i, 0)), full(g), full(b)],
        out_specs=pl.BlockSpec((tile, D_MODEL), lambda i, e: (i, 0)),
        out_shape=jax.ShapeDtypeStruct((n, D_MODEL), F32),
        scratch_shapes=[pltpu.VMEM((D_MODEL, tile), F32), pltpu.VMEM((2, etile, tile), F32),
                        pltpu.VMEM((etile, tile), BF16), pltpu.SMEM((1,), jnp.int32)],
        compiler_params=_params(("parallel", "arbitrary")),
        name="peer_dense",
    )(xt, u, vt, *route, x2d, g, b)


def _rope_tables(seq_len):
    pos = jnp.arange(seq_len, dtype=F32)
    inv_freq = ROPE_THETA ** (-jnp.arange(0, MLA_ROPE_DIM, 2, dtype=F32) / MLA_ROPE_DIM)
    ang = pos[:, None] * inv_freq[None, :]
    cos, sin = jnp.cos(ang), jnp.sin(ang)
    ones = jnp.ones((seq_len, MLA_NOPE_DIM), F32)
    pad = jnp.zeros((seq_len, LANES - MLA_QK_DIM), F32)
    cos_tab = jnp.concatenate([ones, cos, cos, pad], axis=1)
    sin_tab = jnp.concatenate([0.0 * ones, -sin, sin, pad], axis=1)
    return cos_tab, sin_tab


def _rotate_half_columns(w):
    half = MLA_ROPE_DIM // 2
    return jnp.concatenate([w[..., half:], w[..., :half]], axis=-1)


def _mixer_weights(w_in, w_uq, w_ukv):
    sb_scale = 1.0 / math.sqrt(SB_HEAD_DIM)
    wsb = jnp.concatenate([w_in[:, :D_SB] * sb_scale, w_in[:, D_SB:3 * D_SB]], axis=1).astype(BF16)
    wc = w_in[:, 3 * D_SB:3 * D_SB + MLA_Q_RANK + MLA_KV_RANK].astype(BF16)
    w_kr = w_in[:, 3 * D_SB + MLA_Q_RANK + MLA_KV_RANK:]
    lead = jnp.zeros((D_MODEL, MLA_NOPE_DIM), F32)
    trail = jnp.zeros((D_MODEL, LANES - MLA_QK_DIM), F32)
    wkr = jnp.concatenate([lead, w_kr, trail, lead, _rotate_half_columns(w_kr), trail], axis=1).astype(BF16)

    wq = w_uq.reshape(MLA_Q_RANK, MLA_HEADS, MLA_QK_DIM)
    qpad = jnp.zeros((MLA_Q_RANK, MLA_HEADS, LANES - MLA_QK_DIM), F32)
    q_plain = jnp.concatenate([wq, qpad], axis=2)
    q_rot = jnp.concatenate([0.0 * wq[..., :MLA_NOPE_DIM], _rotate_half_columns(wq[..., MLA_NOPE_DIM:]), qpad], axis=2)
    wq2 = jnp.concatenate([q_plain.reshape(MLA_Q_RANK, -1), q_rot.reshape(MLA_Q_RANK, -1)], axis=1).astype(BF16)

    wkv = w_ukv.reshape(MLA_KV_RANK, MLA_HEADS, MLA_NOPE_DIM + MLA_V_DIM)
    kpad = jnp.zeros((MLA_KV_RANK, MLA_HEADS, LANES - MLA_NOPE_DIM), F32)
    wk = jnp.concatenate([wkv[..., :MLA_NOPE_DIM], kpad], axis=2).reshape(MLA_KV_RANK, -1)
    wv = wkv[..., MLA_NOPE_DIM:].reshape(MLA_KV_RANK, -1)
    wkv2 = jnp.concatenate([wk, wv], axis=1).astype(BF16)
    return wsb, wc, wkr, wq2, wkv2


def kernel(x, w_in, g_cq, w_uq, g_ckv, w_ukv, w_o, ln1_g, ln1_b, peer_wq, peer_keys, peer_u, peer_v, ln2_g, ln2_b):
    b, s, d = x.shape
    n = b * s
    cos_tab, sin_tab = _rope_tables(s)
    tile = min(ATTN_TILE, s)
    strict_upper = (jnp.arange(tile)[:, None] > jnp.arange(tile)[None, :]).astype(BF16)
    _, cand_ids = _candidate_tables()
    cand_ids = jnp.asarray(np.broadcast_to(cand_ids[:, None], (N_CAND, LANES)).copy())

    x2d = x.reshape(n, d)
    for l in range(DEPTH):
        wsb, wc, wkr, wq2, wkv2 = _mixer_weights(w_in[l], w_uq[l], w_ukv[l])
        sbq, sbk, sbv, mq, mk, mvt = _in_proj(x2d, s, wsb, wc, wkr, g_cq[l][None, :], g_ckv[l][None, :], wq2, wkv2,
                                              cos_tab, sin_tab)
        sh = lambda a: a.reshape(b, s, a.shape[-1])
        o_sb = _sb_attention(sh(sbq), sh(sbk), sh(sbv), strict_upper)
        o_mla = _mla_attention(sh(mq), sh(mk), mvt)
        x2d = _out_proj_ln(o_sb.reshape(n, D_SB), o_mla.reshape(n, D_MLA_OUT), x2d, w_o[l].astype(BF16),
                           ln1_g[l][None, :], ln1_b[l][None, :])
        wqt = peer_wq[l].T.astype(BF16)
        keys = peer_keys[l].reshape(PEER_HEADS * 2, PEER_N_KEYS, PEER_HALF).astype(BF16)
        route = _peer_route(x2d, wqt, keys, cand_ids)
        x2d = _peer_dense(route[0], peer_u[l].astype(BF16), peer_v[l].T.astype(BF16), route[1:], x2d,
                          ln2_g[l][None, :], ln2_b[l][None, :])
    return x2d.reshape(b, s, d)
```

```python
import functools
import math

import jax
import jax.numpy as jnp
import numpy as np
from jax import lax
from jax.experimental import pallas as pl
from jax.experimental.pallas import tpu as pltpu

D_MODEL = 1024
DEPTH = 2
SB_HEADS = 8
SB_HEAD_DIM = 64
D_SB = SB_HEADS * SB_HEAD_DIM
MLA_HEADS = 8
MLA_NOPE_DIM = 64
MLA_ROPE_DIM = 32
MLA_QK_DIM = MLA_NOPE_DIM + MLA_ROPE_DIM
MLA_V_DIM = 64
MLA_Q_RANK = 256
MLA_KV_RANK = 128
D_MLA_OUT = MLA_HEADS * MLA_V_DIM
ROPE_THETA = 10000.0
PEER_HEADS = 8
PEER_N_KEYS = 128
PEER_N_EXPERTS = PEER_N_KEYS * PEER_N_KEYS
PEER_HALF = 128
PEER_TOPK = 16
DEEPNORM_ALPHA = (2.0 * DEPTH) ** 0.25
LN_EPS = 1e-5
RMS_EPS = 1e-6

LANES = 128
SUBLANES = 8
VMEM_LIMIT_BYTES = 56 * 1024 * 1024

TOKEN_TILE = 512
ATTN_TILE = 256
MLA_TILE = 1024
ROUTE_TILE = 512
DENSE_TOKEN_TILE = 256
DENSE_EXPERT_TILE = 1024
GATE_ROWS = 16

EXP_UNDERFLOW = -120.0
NOT_RANKED = 4096.0
N_CAND = 56

F32 = jnp.float32
BF16 = jnp.bfloat16


def _dot(a, b):
    return jnp.dot(a, b, preferred_element_type=F32)


def _dot_nt(a, b):
    return lax.dot_general(a, b, (((1,), (1,)), ((), ())), preferred_element_type=F32)


def _params(semantics, flags=None):
    return pltpu.CompilerParams(dimension_semantics=semantics, vmem_limit_bytes=VMEM_LIMIT_BYTES, flags=flags)


def _in_proj_kernel(x_ref, wsb_ref, wc_ref, wkr_ref, gq_ref, gkv_ref, wq2_ref, wkv_ref, cos_ref, sin_ref,
                    sbq_ref, sbk_ref, sbv_ref, mq_ref, mk_ref, mv_ref):
    xb = x_ref[...].astype(BF16)
    sb = _dot(xb, wsb_ref[...])
    sbq_ref[...] = sb[:, :D_SB].astype(BF16)
    sbk_ref[...] = sb[:, D_SB:2 * D_SB].astype(BF16)
    sbv_ref[...] = sb[:, 2 * D_SB:].astype(BF16)

    c = _dot(xb, wc_ref[...])
    cq = c[:, :MLA_Q_RANK]
    ckv = c[:, MLA_Q_RANK:]
    nq = cq * lax.rsqrt(jnp.mean(cq * cq, axis=-1, keepdims=True) + RMS_EPS) * gq_ref[...]
    nkv = ckv * lax.rsqrt(jnp.mean(ckv * ckv, axis=-1, keepdims=True) + RMS_EPS) * gkv_ref[...]

    cos = cos_ref[...]
    sin = sin_ref[...]
    kr = _dot(xb, wkr_ref[...])
    k_rope = kr[:, :LANES] * cos + kr[:, LANES:] * sin

    q2 = _dot(nq.astype(BF16), wq2_ref[...])
    kv = _dot(nkv.astype(BF16), wkv_ref[...])
    scale = 1.0 / math.sqrt(MLA_QK_DIM)
    for h in range(MLA_HEADS):
        lo, hi = h * LANES, (h + 1) * LANES
        qa = q2[:, lo:hi]
        qb = q2[:, MLA_HEADS * LANES + lo:MLA_HEADS * LANES + hi]
        mq_ref[:, lo:hi] = ((qa * cos + qb * sin) * scale).astype(BF16)
        mk_ref[:, lo:hi] = (kv[:, lo:hi] + k_rope).astype(BF16)
    mv_ref[...] = kv[:, MLA_HEADS * LANES:].T.astype(BF16)


def _in_proj(x2d, seq_len, wsb, wc, wkr, gq, gkv, wq2, wkv, cos_tab, sin_tab):
    n = x2d.shape[0]
    tm = min(TOKEN_TILE, seq_len)
    pos_tiles = seq_len // tm
    full = lambda a: pl.BlockSpec(a.shape, lambda i: (0,) * a.ndim)
    row = lambda w: pl.BlockSpec((tm, w), lambda i: (i, 0))
    col = lambda r: pl.BlockSpec((r, tm), lambda i: (0, i))
    pos = pl.BlockSpec((tm, LANES), lambda i: (i % pos_tiles, 0))
    wide = MLA_HEADS * LANES
    outs = [(n, D_SB), (n, D_SB), (n, D_SB), (n, wide), (n, wide), (D_MLA_OUT, n)]
    out_specs = [row(D_SB), row(D_SB), row(D_SB), row(wide), row(wide), col(D_MLA_OUT)]
    return pl.pallas_call(
        _in_proj_kernel,
        grid=(n // tm,),
        in_specs=[row(D_MODEL), full(wsb), full(wc), full(wkr), full(gq), full(gkv), full(wq2), full(wkv), pos, pos],
        out_specs=out_specs,
        out_shape=[jax.ShapeDtypeStruct(s, BF16) for s in outs],
        compiler_params=_params(("parallel",)),
        name="in_proj",
    )(x2d, wsb, wc, wkr, gq, gkv, wq2, wkv, cos_tab, sin_tab)


def _sb_attn_kernel(q_ref, k_ref, v_ref, u_ref, o_ref, acc_ref, c_ref, *, tile):
    qi = pl.program_id(2)
    q2 = q_ref[0]
    lane = lax.broadcasted_iota(jnp.int32, (tile, LANES), 1)
    zero = jnp.zeros_like(q2)
    q_heads = (jnp.where(lane < SB_HEAD_DIM, q2, zero), jnp.where(lane >= SB_HEAD_DIM, q2, zero))
    u = u_ref[...]
    r_id = lax.broadcasted_iota(jnp.int32, (tile, tile), 0)
    c_id = lax.broadcasted_iota(jnp.int32, (tile, tile), 1)
    causal = c_id < r_id

    acc_ref[...] = jnp.zeros_like(acc_ref)
    c_ref[...] = jnp.zeros_like(c_ref)

    def block(j, masked):
        start = pl.multiple_of(j * tile, tile)
        kb = k_ref[0, pl.ds(start, tile), :]
        vb = v_ref[0, pl.ds(start, tile), :]
        for hh in range(2):
            z = _dot_nt(q_heads[hh], kb)
            softplus = jnp.maximum(z, 0.0) + jnp.log1p(jnp.exp(-jnp.abs(z)))
            log_beta = z - softplus
            log_1m = -softplus
            if masked:
                log_1m = jnp.where(causal, log_1m, 0.0)
            hi = log_1m.astype(BF16)
            lo = (log_1m - hi.astype(F32)).astype(BF16)
            tail = _dot(hi, u) + _dot(lo, u) + c_ref[hh]
            w = jnp.exp(log_beta + tail)
            if masked:
                w = jnp.where(causal, w, 0.0)
            acc_ref[hh] += _dot(w.astype(BF16), vb)
            c_ref[hh] += jnp.sum(log_1m, axis=1, keepdims=True)

    block(qi, True)

    def more(carry):
        t, c_max = carry
        return jnp.logical_and(t < qi, c_max > EXP_UNDERFLOW)

    def body(carry):
        t, _ = carry
        block(qi - 1 - t, False)
        return t + 1, jnp.max(c_ref[...])

    lax.while_loop(more, body, (jnp.int32(0), jnp.max(c_ref[...])))
    o_ref[0] = jnp.where(lane < SB_HEAD_DIM, acc_ref[0], acc_ref[1]).astype(o_ref.dtype)


def _sb_attention(q, k, v, u):
    b, s, _ = q.shape
    tile = min(ATTN_TILE, s)
    pairs = D_SB // LANES
    q_spec = pl.BlockSpec((1, tile, LANES), lambda bi, p, i: (bi, i, p))
    kv_spec = pl.BlockSpec((1, s, LANES), lambda bi, p, i: (bi, 0, p))
    return pl.pallas_call(
        functools.partial(_sb_attn_kernel, tile=tile),
        grid=(b, pairs, s // tile),
        in_specs=[q_spec, kv_spec, kv_spec, pl.BlockSpec((tile, tile), lambda bi, p, i: (0, 0))],
        out_specs=q_spec,
        out_shape=jax.ShapeDtypeStruct((b, s, D_SB), BF16),
        scratch_shapes=[pltpu.VMEM((2, tile, LANES), F32), pltpu.VMEM((2, tile, 1), F32)],
        compiler_params=_params(("parallel", "parallel", "arbitrary")),
        name="sb_attn",
    )(q, k, v, u)


def _mla_attn_kernel(q_ref, k_ref, vt_ref, o_ref, m_ref, l_ref, acc_ref, *, tile):
    qi = pl.program_id(2)
    q2 = q_ref[0]
    k_id = lax.broadcasted_iota(jnp.int32, (tile, tile), 0)
    q_id = lax.broadcasted_iota(jnp.int32, (tile, tile), 1)
    causal = k_id <= q_id

    m_ref[...] = jnp.full_like(m_ref, -jnp.inf)
    l_ref[...] = jnp.zeros_like(l_ref)
    acc_ref[...] = jnp.zeros_like(acc_ref)

    def block(j, masked):
        start = pl.multiple_of(j * tile, tile)
        kb = k_ref[0, pl.ds(start, tile), :]
        vtb = vt_ref[:, pl.ds(start, tile)]
        for hh in range(2):
            s = _dot_nt(kb[:, hh * LANES:(hh + 1) * LANES], q2[:, hh * LANES:(hh + 1) * LANES])
            if masked:
                s = jnp.where(causal, s, -jnp.inf)
            m_old = m_ref[hh]
            m_new = jnp.maximum(m_old, jnp.max(s, axis=0, keepdims=True))
            p = jnp.exp(s - m_new)
            alpha = jnp.exp(m_old - m_new)
            l_ref[hh] = alpha * l_ref[hh] + jnp.sum(p, axis=0, keepdims=True)
            pv = _dot(vtb[hh * MLA_V_DIM:(hh + 1) * MLA_V_DIM, :], p.astype(BF16))
            acc_ref[hh] = alpha * acc_ref[hh] + pv
            m_ref[hh] = m_new

    block(qi, True)

    def body(j, carry):
        block(j, False)
        return carry

    lax.fori_loop(0, qi, body, 0)
    o_t = jnp.concatenate([acc_ref[0] / l_ref[0], acc_ref[1] / l_ref[1]], axis=0)
    o_ref[0] = o_t.T.astype(o_ref.dtype)


def _mla_attention(q, k, vt):
    b, s, _ = q.shape
    tile = min(MLA_TILE, s)
    pairs = MLA_HEADS // 2
    q_spec = pl.BlockSpec((1, tile, 2 * LANES), lambda bi, p, i: (bi, i, p))
    k_spec = pl.BlockSpec((1, s, 2 * LANES), lambda bi, p, i: (bi, 0, p))
    vt_spec = pl.BlockSpec((LANES, s), lambda bi, p, i: (p, bi))
    o_spec = pl.BlockSpec((1, tile, LANES), lambda bi, p, i: (bi, i, p))
    return pl.pallas_call(
        functools.partial(_mla_attn_kernel, tile=tile),
        grid=(b, pairs, s // tile),
        in_specs=[q_spec, k_spec, vt_spec],
        out_specs=o_spec,
        out_shape=jax.ShapeDtypeStruct((b, s, D_MLA_OUT), BF16),
        scratch_shapes=[pltpu.VMEM((2, 1, tile), F32), pltpu.VMEM((2, 1, tile), F32),
                        pltpu.VMEM((2, MLA_V_DIM, tile), F32)],
        compiler_params=_params(("parallel", "parallel", "arbitrary")),
        name="mla_attn",
    )(q, k, vt)


def _layer_norm(y, g, b):
    mu = jnp.mean(y, axis=-1, keepdims=True)
    d = y - mu
    var = jnp.mean(d * d, axis=-1, keepdims=True)
    return d * lax.rsqrt(var + LN_EPS) * g + b


def _out_proj_kernel(osb_ref, omla_ref, x_ref, wo_ref, g_ref, b_ref, y_ref):
    mix = _dot(osb_ref[...], wo_ref[:D_SB, :]) + _dot(omla_ref[...], wo_ref[D_SB:, :])
    y_ref[...] = _layer_norm(DEEPNORM_ALPHA * x_ref[...] + mix, g_ref[...], b_ref[...])


def _out_proj_ln(osb, omla, x2d, wo, g, b):
    n = x2d.shape[0]
    tm = min(TOKEN_TILE, n)
    full = lambda a: pl.BlockSpec(a.shape, lambda i: (0,) * a.ndim)
    row = lambda w: pl.BlockSpec((tm, w), lambda i: (i, 0))
    return pl.pallas_call(
        _out_proj_kernel,
        grid=(n // tm,),
        in_specs=[row(D_SB), row(D_MLA_OUT), row(D_MODEL), full(wo), full(g), full(b)],
        out_specs=row(D_MODEL),
        out_shape=jax.ShapeDtypeStruct((n, D_MODEL), F32),
        compiler_params=_params(("parallel",)),
        name="out_proj_ln",
    )(osb, omla, x2d, wo, g, b)


def _candidate_tables():
    segs = [(a, PEER_TOPK // (a + 1)) for a in range(8)]
    ids = []
    for a, nb in segs:
        ids += [a * PEER_TOPK + b for b in range(nb)]
    ids += [a * PEER_TOPK for a in range(8, PEER_TOPK)]
    ids += [1000.0, 1001.0]
    ids += [2000.0 + i for i in range(N_CAND - len(ids))]
    return segs, np.asarray(ids, np.float32)


def _extract_top(s, n_take, n_rank, row_id, id_sentinel):
    rank = jnp.full(s.shape, NOT_RANKED, F32)
    vals, ids = [], []
    for r in range(n_take):
        m = jnp.max(s, axis=0, keepdims=True)
        i = jnp.min(jnp.where(s == m, row_id, id_sentinel), axis=0, keepdims=True)
        hit = row_id == i
        s = jnp.where(hit, -jnp.inf, s)
        if r < n_rank:
            rank = jnp.where(hit, float(r), rank)
        vals.append(m)
        ids.append(i)
    return vals, ids, rank


def _extract_values(s, n_take):
    vals = []
    for _ in range(n_take):
        m = jnp.max(s, axis=0, keepdims=True)
        s = jnp.where(s == m, -jnp.inf, s)
        vals.append(m)
    removed = jnp.sum(jnp.where(s == -jnp.inf, 1.0, 0.0), axis=0, keepdims=True)
    return vals, jnp.where(removed == float(n_take), 0.0, 1.0)


def _route_kernel(x_ref, wqt_ref, keys_ref, cid_ref,
                  xt_ref, sa_ref, sb_ref, e1_ref, e2_ref, th_ref, ra_ref, rb_ref, tau_ref, ct_ref, flag_ref,
                  qt_ref, top_ref, cand_ref, *, tile, segs):
    xb = x_ref[...].astype(BF16)
    xt_ref[...] = xb.T
    qt_ref[...] = _dot_nt(wqt_ref[...], xb).astype(BF16)
    key_id = lax.broadcasted_iota(jnp.int32, (PEER_N_KEYS, LANES), 0).astype(F32)
    cand_id = cid_ref[...]

    def head(h, carry):
        rows = pl.ds(pl.multiple_of(h * PEER_N_KEYS, PEER_N_KEYS), PEER_N_KEYS)
        slab = pl.ds(pl.multiple_of(h * SUBLANES, SUBLANES), SUBLANES)

        def select(g, s0, s1, ranks):
            top0, top1, cand = top_ref.at[g, 0], top_ref.at[g, 1], cand_ref.at[g]
            off = 0
            for a, nb in segs:
                cand[off:off + nb, :] = top0[a:a + 1, :] + top1[0:nb, :]
                off += nb
            cand[off:off + 8, :] = top0[8:16, :] + top1[0:1, :]
            off += 8
            cand[off:off + 1, :] = top0[16:17, :] + top1[0:1, :]
            cand[off + 1:off + 2, :] = top0[0:1, :] + top1[16:17, :]
            cand[off + 2:, :] = jnp.full((N_CAND - off - 2, LANES), -jnp.inf, F32)
            sums, cids, _ = _extract_top(cand[...], PEER_TOPK + 1, 0, cand_id, 4000.0)
            z = jnp.zeros_like(sums[0])
            for r in range(PEER_TOPK):
                z = z + jnp.exp(sums[r] - sums[0])
            tau = sums[PEER_TOPK - 1]
            e1_ref[g, rows, :] = jnp.exp(s0 - top0[0:1, :]) / z
            e2_ref[g, rows, :] = jnp.exp(s1 - top1[0:1, :])
            vb = top1[0:PEER_TOPK, :]
            theta = jnp.full((PEER_N_KEYS, LANES), jnp.inf, F32)
            for a in range(PEER_TOPK):
                va = top0[a:a + 1, :]
                th_a = jnp.min(jnp.where(va + vb >= tau, vb, jnp.inf), axis=0, keepdims=True)
                match = (s0 == va) if ranks is None else (ranks[0] == float(a))
                theta = jnp.where(match, th_a, theta)
            th_ref[g, rows, :] = theta
            tied = jnp.where(sums[PEER_TOPK] == tau, 1.0, 0.0)
            tau_ref[g, slab, :] = jnp.broadcast_to(tau, (SUBLANES, LANES))
            ct_ref[g, slab, :] = jnp.broadcast_to(cids[PEER_TOPK - 1], (SUBLANES, LANES))
            flag_ref[g, slab, :] = jnp.broadcast_to(tied, (SUBLANES, LANES))
            return tied

        redo = []
        for g in range(tile // LANES):
            lanes = slice(g * LANES, (g + 1) * LANES)
            scores = []
            repeated = jnp.zeros((1, LANES), F32)
            for c in range(2):
                hc = h * 2 + c
                q = qt_ref[pl.ds(pl.multiple_of(hc * PEER_HALF, PEER_HALF), PEER_HALF), lanes]
                s = _dot(keys_ref[hc], q)
                vals, rep = _extract_values(s, PEER_TOPK + 1)
                for r, v in enumerate(vals):
                    top_ref[g, c, r:r + 1, :] = v
                repeated = jnp.maximum(repeated, rep)
                scores.append(s)
            s0, s1 = scores
            sa_ref[g, rows, :] = s0
            sb_ref[g, rows, :] = s1
            ra_ref[g, rows, :] = jnp.full((PEER_N_KEYS, LANES), NOT_RANKED, F32)
            rb_ref[g, rows, :] = jnp.full((PEER_N_KEYS, LANES), NOT_RANKED, rb_ref.dtype)
            tied = select(g, s0, s1, None)
            redo.append(jnp.max(jnp.maximum(repeated, tied)) > 0.0)

        def exact_group(g):
            ranks = []
            for c, sc in enumerate((sa_ref[g, rows, :], sb_ref[g, rows, :])):
                vals, _, rank = _extract_top(sc, PEER_TOPK + 1, PEER_TOPK, key_id, float(PEER_N_KEYS))
                for r, v in enumerate(vals):
                    top_ref[g, c, r:r + 1, :] = v
                ranks.append(rank)
            ra_ref[g, rows, :] = ranks[0] * float(PEER_TOPK)
            rb_ref[g, rows, :] = ranks[1].astype(rb_ref.dtype)
            select(g, sa_ref[g, rows, :], sb_ref[g, rows, :], ranks)

        for g in range(tile // LANES):
            pl.when(redo[g])(functools.partial(exact_group, g))
        return carry

    lax.fori_loop(0, PEER_HEADS, head, 0)


def _peer_route(x2d, wqt, keys, cand_ids):
    n = x2d.shape[0]
    tile = min(ROUTE_TILE, n)
    segs, _ = _candidate_tables()
    hk = PEER_HEADS * PEER_N_KEYS
    full = lambda a: pl.BlockSpec(a.shape, lambda i: (0,) * a.ndim)
    groups = tile // LANES
    slab = lambda r: pl.BlockSpec((groups, r, LANES), lambda i: (i, 0, 0))
    slab_shape = lambda r: (n // LANES, r, LANES)
    hs = PEER_HEADS * SUBLANES
    outs = ([((D_MODEL, n), BF16)] + [(slab_shape(hk), F32)] * 6 + [(slab_shape(hk), BF16)]
            + [(slab_shape(hs), F32)] * 3)
    out_specs = ([pl.BlockSpec((D_MODEL, tile), lambda i: (0, i))] + [slab(hk)] * 7 + [slab(hs)] * 3)
    return pl.pallas_call(
        functools.partial(_route_kernel, tile=tile, segs=segs),
        grid=(n // tile,),
        in_specs=[pl.BlockSpec((tile, D_MODEL), lambda i: (i, 0)), full(wqt), full(keys), full(cand_ids)],
        out_specs=out_specs,
        out_shape=[jax.ShapeDtypeStruct(s, d) for s, d in outs],
        scratch_shapes=[pltpu.VMEM((2 * PEER_HEADS * PEER_HALF, tile), BF16),
                        pltpu.VMEM((groups, 2, 24, LANES), F32),
                        pltpu.VMEM((groups, N_CAND, LANES), F32)],
        compiler_params=_params(("parallel",)),
        name="peer_route",
    )(x2d, wqt, keys, cand_ids)


def _gelu(h):
    return 0.5 * h * (1.0 + lax.erf(h * (1.0 / math.sqrt(2.0))))


def _dense_kernel(xt_ref, u_ref, vt_ref, sa_ref, sb_ref, e1_ref, e2_ref, th_ref, ra_ref, rb_ref, tau_ref, ct_ref,
                  flag_ref, x_ref, g_ref, b_ref, y_ref, acc_ref, h_ref, w_ref, tied_ref, *, tile, etile):
    s = pl.program_id(1)
    n_tiles = pl.num_programs(1) - 1

    @pl.when(s == 0)
    def _():
        acc_ref[...] = jnp.zeros_like(acc_ref)
        tied_ref[0] = (jnp.max(flag_ref[...]) > 0.0).astype(jnp.int32)
        h_ref[0] = _dot(u_ref[...], xt_ref[...])

    tied = tied_ref[0] > 0

    def step(parity, exact):
        cur, nxt = 1 - parity, parity
        h_ref[nxt] = _dot(u_ref[...], xt_ref[...])
        down = None
        for r in range(etile // PEER_N_KEYS):
            i0 = (s - 1) * (etile // PEER_N_KEYS) + r
            for g in range(tile // LANES):
                lanes = slice(g * LANES, (g + 1) * LANES)
                row_of = lambda ref, h: jnp.broadcast_to(ref[g, pl.ds(h * PEER_N_KEYS + i0, 1), :], (SUBLANES, LANES))
                head_of = lambda ref, h: ref[g, h * SUBLANES:(h + 1) * SUBLANES, :]
                e1_b = [row_of(e1_ref, h) for h in range(PEER_HEADS)]
                if exact:
                    a_b = [row_of(sa_ref, h) for h in range(PEER_HEADS)]
                    ra_b = [row_of(ra_ref, h) for h in range(PEER_HEADS)]
                    tau_b = [head_of(tau_ref, h) for h in range(PEER_HEADS)]
                    ct_b = [head_of(ct_ref, h) for h in range(PEER_HEADS)]
                else:
                    th_b = [row_of(th_ref, h) for h in range(PEER_HEADS)]
                for jc in range(PEER_N_KEYS // GATE_ROWS):
                    if exact:
                        rb = [rb_ref[g, h * PEER_N_KEYS + jc * GATE_ROWS:h * PEER_N_KEYS + (jc + 1) * GATE_ROWS, :]
                              .astype(F32) for h in range(PEER_HEADS)]
                    parts = []
                    for sub in range(GATE_ROWS // SUBLANES):
                        j0 = jc * GATE_ROWS + sub * SUBLANES
                        gate = None
                        for h in range(PEER_HEADS):
                            rows = slice(h * PEER_N_KEYS + j0, h * PEER_N_KEYS + j0 + SUBLANES)
                            if exact:
                                total = a_b[h] + sb_ref[g, rows, :]
                                cand = ra_b[h] + rb[h][sub * SUBLANES:(sub + 1) * SUBLANES]
                                keep = (total > tau_b[h]) | ((total == tau_b[h]) & (cand <= ct_b[h]))
                            else:
                                keep = sb_ref[g, rows, :] >= th_b[h]
                            term = jnp.where(keep, e1_b[h] * e2_ref[g, rows, :], 0.0)
                            gate = term if gate is None else gate + term
                        hv = h_ref[cur, r * PEER_N_KEYS + j0:r * PEER_N_KEYS + j0 + SUBLANES, lanes]
                        parts.append(gate * _gelu(hv))
                    rows = slice(r * PEER_N_KEYS + jc * GATE_ROWS, r * PEER_N_KEYS + (jc + 1) * GATE_ROWS)
                    w_ref[rows, lanes] = jnp.concatenate(parts, axis=0).astype(BF16)
            block = slice(r * PEER_N_KEYS, (r + 1) * PEER_N_KEYS)
            part = _dot(vt_ref[:, block], w_ref[block, :])
            down = part if down is None else down + part
        acc_ref[...] += down

    for parity in range(2):
        for exact in (False, True):
            cond = jnp.logical_and(jnp.logical_and(s >= 1, lax.rem(s, 2) == parity), tied == exact)
            pl.when(cond)(functools.partial(step, parity, exact))

    @pl.when(s == n_tiles)
    def _():
        y = DEEPNORM_ALPHA * x_ref[...] + acc_ref[...].T
        y_ref[...] = _layer_norm(y, g_ref[...], b_ref[...])


def _peer_dense(xt, u, vt, route, x2d, g, b):
    n = x2d.shape[0]
    tile = min(DENSE_TOKEN_TILE, n)
    etile = DENSE_EXPERT_TILE
    steps = PEER_N_EXPERTS // etile
    hk = PEER_HEADS * PEER_N_KEYS
    slab = lambda r: pl.BlockSpec((tile // LANES, r, LANES), lambda i, e: (i, 0, 0))
    full = lambda a: pl.BlockSpec(a.shape, lambda i, e: (0,) * a.ndim)
    return pl.pallas_call(
        functools.partial(_dense_kernel, tile=tile, etile=etile),
        grid=(n // tile, steps + 1),
        in_specs=[pl.BlockSpec((D_MODEL, tile), lambda i, e: (0, i)),
                  pl.BlockSpec((etile, D_MODEL), lambda i, e: (jnp.minimum(e, steps - 1), 0)),
                  pl.BlockSpec((D_MODEL, etile), lambda i, e: (0, jnp.maximum(e - 1, 0)))]
                 + [slab(hk)] * 7 + [slab(PEER_HEADS * SUBLANES)] * 3
                 + [pl.BlockSpec((tile, D_MODEL), lambda i, e: (i, 0)), full(g), full(b)],
        out_specs=pl.BlockSpec((tile, D_MODEL), lambda i, e: (i, 0)),
        out_shape=jax.ShapeDtypeStruct((n, D_MODEL), F32),
        scratch_shapes=[pltpu.VMEM((D_MODEL, tile), F32), pltpu.VMEM((2, etile, tile), F32),
                        pltpu.VMEM((etile, tile), BF16), pltpu.SMEM((1,), jnp.int32)],
        compiler_params=_params(("parallel", "arbitrary")),
        name="peer_dense",
    )(xt, u, vt, *route, x2d, g, b)


def _rope_tables(seq_len):
    pos = jnp.arange(seq_len, dtype=F32)
    inv_freq = ROPE_THETA ** (-jnp.arange(0, MLA_ROPE_DIM, 2, dtype=F32) / MLA_ROPE_DIM)
    ang = pos[:, None] * inv_freq[None, :]
    cos, sin = jnp.cos(ang), jnp.sin(ang)
    ones = jnp.ones((seq_len, MLA_NOPE_DIM), F32)
    pad = jnp.zeros((seq_len, LANES - MLA_QK_DIM), F32)
    cos_tab = jnp.concatenate([ones, cos, cos, pad], axis=1)
    sin_tab = jnp.concatenate([0.0 * ones, -sin, sin, pad], axis=1)
    return cos_tab, sin_tab


def _rotate_half_columns(w):
    half = MLA_ROPE_DIM // 2
    return jnp.concatenate([w[..., half:], w[..., :half]], axis=-1)


def _mixer_weights(w_in, w_uq, w_ukv):
    sb_scale = 1.0 / math.sqrt(SB_HEAD_DIM)
    wsb = jnp.concatenate([w_in[:, :D_SB] * sb_scale, w_in[:, D_SB:3 * D_SB]], axis=1).astype(BF16)
    wc = w_in[:, 3 * D_SB:3 * D_SB + MLA_Q_RANK + MLA_KV_RANK].astype(BF16)
    w_kr = w_in[:, 3 * D_SB + MLA_Q_RANK + MLA_KV_RANK:]
    lead = jnp.zeros((D_MODEL, MLA_NOPE_DIM), F32)
    trail = jnp.zeros((D_MODEL, LANES - MLA_QK_DIM), F32)
    wkr = jnp.concatenate([lead, w_kr, trail, lead, _rotate_half_columns(w_kr), trail], axis=1).astype(BF16)

    wq = w_uq.reshape(MLA_Q_RANK, MLA_HEADS, MLA_QK_DIM)
    qpad = jnp.zeros((MLA_Q_RANK, MLA_HEADS, LANES - MLA_QK_DIM), F32)
    q_plain = jnp.concatenate([wq, qpad], axis=2)
    q_rot = jnp.concatenate([0.0 * wq[..., :MLA_NOPE_DIM], _rotate_half_columns(wq[..., MLA_NOPE_DIM:]), qpad], axis=2)
    wq2 = jnp.concatenate([q_plain.reshape(MLA_Q_RANK, -1), q_rot.reshape(MLA_Q_RANK, -1)], axis=1).astype(BF16)

    wkv = w_ukv.reshape(MLA_KV_RANK, MLA_HEADS, MLA_NOPE_DIM + MLA_V_DIM)
    kpad = jnp.zeros((MLA_KV_RANK, MLA_HEADS, LANES - MLA_NOPE_DIM), F32)
    wk = jnp.concatenate([wkv[..., :MLA_NOPE_DIM], kpad], axis=2).reshape(MLA_KV_RANK, -1)
    wv = wkv[..., MLA_NOPE_DIM:].reshape(MLA_KV_RANK, -1)
    wkv2 = jnp.concatenate([wk, wv], axis=1).astype(BF16)
    return wsb, wc, wkr, wq2, wkv2


def kernel(x, w_in, g_cq, w_uq, g_ckv, w_ukv, w_o, ln1_g, ln1_b, peer_wq, peer_keys, peer_u, peer_v, ln2_g, ln2_b):
    b, s, d = x.shape
    n = b * s
    cos_tab, sin_tab = _rope_tables(s)
    tile = min(ATTN_TILE, s)
    strict_upper = (jnp.arange(tile)[:, None] > jnp.arange(tile)[None, :]).astype(BF16)
    _, cand_ids = _candidate_tables()
    cand_ids = jnp.asarray(np.broadcast_to(cand_ids[:, None], (N_CAND, LANES)).copy())

    x2d = x.reshape(n, d)
    for l in range(DEPTH):
        wsb, wc, wkr, wq2, wkv2 = _mixer_weights(w_in[l], w_uq[l], w_ukv[l])
        sbq, sbk, sbv, mq, mk, mvt = _in_proj(x2d, s, wsb, wc, wkr, g_cq[l][None, :], g_ckv[l][None, :], wq2, wkv2,
                                              cos_tab, sin_tab)
        sh = lambda a: a.reshape(b, s, a.shape[-1])
        o_sb = _sb_attention(sh(sbq), sh(sbk), sh(sbv), strict_upper)
        o_mla = _mla_attention(sh(mq), sh(mk), mvt)
        x2d = _out_proj_ln(o_sb.reshape(n, D_SB), o_mla.reshape(n, D_MLA_OUT), x2d, w_o[l].astype(BF16),
                           ln1_g[l][None, :], ln1_b[l][None, :])
        wqt = peer_wq[l].T.astype(BF16)
        keys = peer_keys[l].reshape(PEER_HEADS * 2, PEER_N_KEYS, PEER_HALF).astype(BF16)
        route = _peer_route(x2d, wqt, keys, cand_ids)
        x2d = _peer_dense(route[0], peer_u[l].astype(BF16), peer_v[l].T.astype(BF16), route[1:], x2d,
                          ln2_g[l][None, :], ln2_b[l][None, :])
    return x2d.reshape(b, s, d)
```

```python
import functools
import math

import jax
import jax.numpy as jnp
import numpy as np
from jax import lax
from jax.experimental import pallas as pl
from jax.experimental.pallas import tpu as pltpu

D_MODEL = 1024
DEPTH = 2
SB_HEADS = 8
SB_HEAD_DIM = 64
D_SB = SB_HEADS * SB_HEAD_DIM
MLA_HEADS = 8
MLA_NOPE_DIM = 64
MLA_ROPE_DIM = 32
MLA_QK_DIM = MLA_NOPE_DIM + MLA_ROPE_DIM
MLA_V_DIM = 64
MLA_Q_RANK = 256
MLA_KV_RANK = 128
D_MLA_OUT = MLA_HEADS * MLA_V_DIM
ROPE_THETA = 10000.0
PEER_HEADS = 8
PEER_N_KEYS = 128
PEER_N_EXPERTS = PEER_N_KEYS * PEER_N_KEYS
PEER_HALF = 128
PEER_TOPK = 16
DEEPNORM_ALPHA = (2.0 * DEPTH) ** 0.25
LN_EPS = 1e-5
RMS_EPS = 1e-6

LANES = 128
SUBLANES = 8
VMEM_LIMIT_BYTES = 56 * 1024 * 1024

TOKEN_TILE = 512
ATTN_TILE = 256
SB_PAIRS_PER_STEP = 1
MLA_TILE = 1024
ROUTE_TILE = 512
DENSE_TOKEN_TILE = 256
DENSE_EXPERT_TILE = 2048
GATE_ROWS = 16

EXP_UNDERFLOW = -120.0
NOT_RANKED = 4096.0
N_CAND = 56

F32 = jnp.float32
BF16 = jnp.bfloat16


def _dot(a, b):
    return jnp.dot(a, b, preferred_element_type=F32)


def _dot_nt(a, b):
    return lax.dot_general(a, b, (((1,), (1,)), ((), ())), preferred_element_type=F32)


def _params(semantics, flags=None):
    return pltpu.CompilerParams(dimension_semantics=semantics, vmem_limit_bytes=VMEM_LIMIT_BYTES, flags=flags)


def _in_proj_kernel(x_ref, wsb_ref, wc_ref, wkr_ref, gq_ref, gkv_ref, wq2_ref, wkv_ref, cos_ref, sin_ref,
                    sbq_ref, sbk_ref, sbv_ref, mq_ref, mk_ref, mv_ref):
    xb = x_ref[...].astype(BF16)
    sb = _dot(xb, wsb_ref[...])
    sbq_ref[...] = sb[:, :D_SB].astype(BF16)
    sbk_ref[...] = sb[:, D_SB:2 * D_SB].astype(BF16)
    sbv_ref[...] = sb[:, 2 * D_SB:].astype(BF16)

    c = _dot(xb, wc_ref[...])
    cq = c[:, :MLA_Q_RANK]
    ckv = c[:, MLA_Q_RANK:]
    nq = cq * lax.rsqrt(jnp.mean(cq * cq, axis=-1, keepdims=True) + RMS_EPS) * gq_ref[...]
    nkv = ckv * lax.rsqrt(jnp.mean(ckv * ckv, axis=-1, keepdims=True) + RMS_EPS) * gkv_ref[...]

    cos = cos_ref[...]
    sin = sin_ref[...]
    kr = _dot(xb, wkr_ref[...])
    k_rope = kr[:, :LANES] * cos + kr[:, LANES:] * sin

    q2 = _dot(nq.astype(BF16), wq2_ref[...])
    kv = _dot(nkv.astype(BF16), wkv_ref[...])
    scale = 1.0 / math.sqrt(MLA_QK_DIM)
    for h in range(MLA_HEADS):
        lo, hi = h * LANES, (h + 1) * LANES
        qa = q2[:, lo:hi]
        qb = q2[:, MLA_HEADS * LANES + lo:MLA_HEADS * LANES + hi]
        mq_ref[:, lo:hi] = ((qa * cos + qb * sin) * scale).astype(BF16)
        mk_ref[:, lo:hi] = (kv[:, lo:hi] + k_rope).astype(BF16)
    mv_ref[...] = kv[:, MLA_HEADS * LANES:].T.astype(BF16)


def _in_proj(x2d, seq_len, wsb, wc, wkr, gq, gkv, wq2, wkv, cos_tab, sin_tab):
    n = x2d.shape[0]
    tm = min(TOKEN_TILE, seq_len)
    pos_tiles = seq_len // tm
    full = lambda a: pl.BlockSpec(a.shape, lambda i: (0,) * a.ndim)
    row = lambda w: pl.BlockSpec((tm, w), lambda i: (i, 0))
    col = lambda r: pl.BlockSpec((r, tm), lambda i: (0, i))
    pos = pl.BlockSpec((tm, LANES), lambda i: (i % pos_tiles, 0))
    wide = MLA_HEADS * LANES
    outs = [(n, D_SB), (n, D_SB), (n, D_SB), (n, wide), (n, wide), (D_MLA_OUT, n)]
    out_specs = [row(D_SB), row(D_SB), row(D_SB), row(wide), row(wide), col(D_MLA_OUT)]
    return pl.pallas_call(
        _in_proj_kernel,
        grid=(n // tm,),
        in_specs=[row(D_MODEL), full(wsb), full(wc), full(wkr), full(gq), full(gkv), full(wq2), full(wkv), pos, pos],
        out_specs=out_specs,
        out_shape=[jax.ShapeDtypeStruct(s, BF16) for s in outs],
        compiler_params=_params(("parallel",)),
        name="in_proj",
    )(x2d, wsb, wc, wkr, gq, gkv, wq2, wkv, cos_tab, sin_tab)


def _sb_attn_kernel(q_ref, k_ref, v_ref, u_ref, o_ref, acc_ref, c_ref, *, tile, pairs):
    qi = pl.program_id(2)
    lane = lax.broadcasted_iota(jnp.int32, (tile, LANES), 1)
    q_heads = []
    for p in range(pairs):
        q2 = q_ref[0, :, p * LANES:(p + 1) * LANES]
        zero = jnp.zeros_like(q2)
        q_heads += [jnp.where(lane < SB_HEAD_DIM, q2, zero), jnp.where(lane >= SB_HEAD_DIM, q2, zero)]
    u = u_ref[...]
    r_id = lax.broadcasted_iota(jnp.int32, (tile, tile), 0)
    c_id = lax.broadcasted_iota(jnp.int32, (tile, tile), 1)
    causal = c_id < r_id

    acc_ref[...] = jnp.zeros_like(acc_ref)
    c_ref[...] = jnp.zeros_like(c_ref)

    def block(j, masked):
        start = pl.multiple_of(j * tile, tile)
        for hh in range(2 * pairs):
            pair = slice((hh // 2) * LANES, (hh // 2 + 1) * LANES)
            kb = k_ref[0, pl.ds(start, tile), pair]
            vb = v_ref[0, pl.ds(start, tile), pair]
            z = _dot_nt(q_heads[hh], kb)
            softplus = jnp.maximum(z, 0.0) + jnp.log1p(jnp.exp(-jnp.abs(z)))
            log_beta = z - softplus
            log_1m = -softplus
            if masked:
                log_1m = jnp.where(causal, log_1m, 0.0)
            hi = log_1m.astype(BF16)
            lo = (log_1m - hi.astype(F32)).astype(BF16)
            tail = _dot(hi, u) + _dot(lo, u) + c_ref[hh]
            w = jnp.exp(log_beta + tail)
            if masked:
                w = jnp.where(causal, w, 0.0)
            acc_ref[hh] += _dot(w.astype(BF16), vb)
            c_ref[hh] += jnp.sum(log_1m, axis=1, keepdims=True)

    block(qi, True)

    def more(carry):
        t, c_max = carry
        return jnp.logical_and(t < qi, c_max > EXP_UNDERFLOW)

    def body(carry):
        t, _ = carry
        block(qi - 1 - t, False)
        return t + 1, jnp.max(c_ref[...])

    lax.while_loop(more, body, (jnp.int32(0), jnp.max(c_ref[...])))
    for p in range(pairs):
        o_pair = jnp.where(lane < SB_HEAD_DIM, acc_ref[2 * p], acc_ref[2 * p + 1])
        o_ref[0, :, p * LANES:(p + 1) * LANES] = o_pair.astype(o_ref.dtype)


def _sb_attention(q, k, v, u):
    b, s, _ = q.shape
    tile = min(ATTN_TILE, s)
    pairs = SB_PAIRS_PER_STEP
    width = pairs * LANES
    q_spec = pl.BlockSpec((1, tile, width), lambda bi, p, i: (bi, i, p))
    kv_spec = pl.BlockSpec((1, s, width), lambda bi, p, i: (bi, 0, p))
    return pl.pallas_call(
        functools.partial(_sb_attn_kernel, tile=tile, pairs=pairs),
        grid=(b, D_SB // width, s // tile),
        in_specs=[q_spec, kv_spec, kv_spec, pl.BlockSpec((tile, tile), lambda bi, p, i: (0, 0))],
        out_specs=q_spec,
        out_shape=jax.ShapeDtypeStruct((b, s, D_SB), BF16),
        scratch_shapes=[pltpu.VMEM((2 * pairs, tile, LANES), F32), pltpu.VMEM((2 * pairs, tile, 1), F32)],
        compiler_params=_params(("parallel", "parallel", "arbitrary")),
        name="sb_attn",
    )(q, k, v, u)


def _mla_attn_kernel(q_ref, k_ref, vt_ref, o_ref, m_ref, l_ref, acc_ref, *, tile):
    qi = pl.program_id(2)
    q2 = q_ref[0]
    k_id = lax.broadcasted_iota(jnp.int32, (tile, tile), 0)
    q_id = lax.broadcasted_iota(jnp.int32, (tile, tile), 1)
    causal = k_id <= q_id

    m_ref[...] = jnp.full_like(m_ref, -jnp.inf)
    l_ref[...] = jnp.zeros_like(l_ref)
    acc_ref[...] = jnp.zeros_like(acc_ref)

    def block(j, masked):
        start = pl.multiple_of(j * tile, tile)
        kb = k_ref[0, pl.ds(start, tile), :]
        vtb = vt_ref[:, pl.ds(start, tile)]
        for hh in range(2):
            s = _dot_nt(kb[:, hh * LANES:(hh + 1) * LANES], q2[:, hh * LANES:(hh + 1) * LANES])
            if masked:
                s = jnp.where(causal, s, -jnp.inf)
            m_old = m_ref[hh]
            m_new = jnp.maximum(m_old, jnp.max(s, axis=0, keepdims=True))
            p = jnp.exp(s - m_new)
            alpha = jnp.exp(m_old - m_new)
            l_ref[hh] = alpha * l_ref[hh] + jnp.sum(p, axis=0, keepdims=True)
            pv = _dot(vtb[hh * MLA_V_DIM:(hh + 1) * MLA_V_DIM, :], p.astype(BF16))
            acc_ref[hh] = alpha * acc_ref[hh] + pv
            m_ref[hh] = m_new

    block(qi, True)

    def body(j, carry):
        block(j, False)
        return carry

    lax.fori_loop(0, qi, body, 0)
    o_t = jnp.concatenate([acc_ref[0] / l_ref[0], acc_ref[1] / l_ref[1]], axis=0)
    o_ref[0] = o_t.T.astype(o_ref.dtype)


def _mla_attention(q, k, vt):
    b, s, _ = q.shape
    tile = min(MLA_TILE, s)
    pairs = MLA_HEADS // 2
    q_spec = pl.BlockSpec((1, tile, 2 * LANES), lambda bi, p, i: (bi, i, p))
    k_spec = pl.BlockSpec((1, s, 2 * LANES), lambda bi, p, i: (bi, 0, p))
    vt_spec = pl.BlockSpec((LANES, s), lambda bi, p, i: (p, bi))
    o_spec = pl.BlockSpec((1, tile, LANES), lambda bi, p, i: (bi, i, p))
    return pl.pallas_call(
        functools.partial(_mla_attn_kernel, tile=tile),
        grid=(b, pairs, s // tile),
        in_specs=[q_spec, k_spec, vt_spec],
        out_specs=o_spec,
        out_shape=jax.ShapeDtypeStruct((b, s, D_MLA_OUT), BF16),
        scratch_shapes=[pltpu.VMEM((2, 1, tile), F32), pltpu.VMEM((2, 1, tile), F32),
                        pltpu.VMEM((2, MLA_V_DIM, tile), F32)],
        compiler_params=_params(("parallel", "parallel", "arbitrary")),
        name="mla_attn",
    )(q, k, vt)


def _layer_norm(y, g, b):
    mu = jnp.mean(y, axis=-1, keepdims=True)
    d = y - mu
    var = jnp.mean(d * d, axis=-1, keepdims=True)
    return d * lax.rsqrt(var + LN_EPS) * g + b


def _out_proj_kernel(osb_ref, omla_ref, x_ref, wo_ref, g_ref, b_ref, y_ref):
    mix = _dot(osb_ref[...], wo_ref[:D_SB, :]) + _dot(omla_ref[...], wo_ref[D_SB:, :])
    y_ref[...] = _layer_norm(DEEPNORM_ALPHA * x_ref[...] + mix, g_ref[...], b_ref[...])


def _out_proj_ln(osb, omla, x2d, wo, g, b):
    n = x2d.shape[0]
    tm = min(TOKEN_TILE, n)
    full = lambda a: pl.BlockSpec(a.shape, lambda i: (0,) * a.ndim)
    row = lambda w: pl.BlockSpec((tm, w), lambda i: (i, 0))
    return pl.pallas_call(
        _out_proj_kernel,
        grid=(n // tm,),
        in_specs=[row(D_SB), row(D_MLA_OUT), row(D_MODEL), full(wo), full(g), full(b)],
        out_specs=row(D_MODEL),
        out_shape=jax.ShapeDtypeStruct((n, D_MODEL), F32),
        compiler_params=_params(("parallel",)),
        name="out_proj_ln",
    )(osb, omla, x2d, wo, g, b)


def _candidate_tables():
    segs = [(a, PEER_TOPK // (a + 1)) for a in range(8)]
    ids = []
    for a, nb in segs:
        ids += [a * PEER_TOPK + b for b in range(nb)]
    ids += [a * PEER_TOPK for a in range(8, PEER_TOPK)]
    ids += [1000.0, 1001.0]
    ids += [2000.0 + i for i in range(N_CAND - len(ids))]
    return segs, np.asarray(ids, np.float32)


def _extract_top(s, n_take, n_rank, row_id, id_sentinel):
    rank = jnp.full(s.shape, NOT_RANKED, F32)
    vals, ids = [], []
    for r in range(n_take):
        m = jnp.max(s, axis=0, keepdims=True)
        i = jnp.min(jnp.where(s == m, row_id, id_sentinel), axis=0, keepdims=True)
        hit = row_id == i
        s = jnp.where(hit, -jnp.inf, s)
        if r < n_rank:
            rank = jnp.where(hit, float(r), rank)
        vals.append(m)
        ids.append(i)
    return vals, ids, rank


def _extract_values(s, n_take):
    vals = []
    for _ in range(n_take):
        m = jnp.max(s, axis=0, keepdims=True)
        s = jnp.where(s == m, -jnp.inf, s)
        vals.append(m)
    removed = jnp.sum(jnp.where(s == -jnp.inf, 1.0, 0.0), axis=0, keepdims=True)
    return vals, jnp.where(removed == float(n_take), 0.0, 1.0)


def _route_kernel(x_ref, wqt_ref, keys_ref, cid_ref,
                  xt_ref, sa_ref, sb_ref, e1_ref, e2_ref, th_ref, ra_ref, rb_ref, tau_ref, ct_ref, flag_ref,
                  qt_ref, top_ref, cand_ref, *, tile, segs):
    xb = x_ref[...].astype(BF16)
    xt_ref[...] = xb.T
    qt_ref[...] = _dot_nt(wqt_ref[...], xb).astype(BF16)
    key_id = lax.broadcasted_iota(jnp.int32, (PEER_N_KEYS, LANES), 0).astype(F32)
    cand_id = cid_ref[...]

    def head(h, carry):
        rows = pl.ds(pl.multiple_of(h * PEER_N_KEYS, PEER_N_KEYS), PEER_N_KEYS)
        slab = pl.ds(pl.multiple_of(h * SUBLANES, SUBLANES), SUBLANES)

        def select(g, s0, s1, ranks):
            top0, top1, cand = top_ref.at[g, 0], top_ref.at[g, 1], cand_ref.at[g]
            off = 0
            for a, nb in segs:
                cand[off:off + nb, :] = top0[a:a + 1, :] + top1[0:nb, :]
                off += nb
            cand[off:off + 8, :] = top0[8:16, :] + top1[0:1, :]
            off += 8
            cand[off:off + 1, :] = top0[16:17, :] + top1[0:1, :]
            cand[off + 1:off + 2, :] = top0[0:1, :] + top1[16:17, :]
            cand[off + 2:, :] = jnp.full((N_CAND - off - 2, LANES), -jnp.inf, F32)
            sums, cids, _ = _extract_top(cand[...], PEER_TOPK + 1, 0, cand_id, 4000.0)
            z = jnp.zeros_like(sums[0])
            for r in range(PEER_TOPK):
                z = z + jnp.exp(sums[r] - sums[0])
            tau = sums[PEER_TOPK - 1]
            e1_ref[g, rows, :] = jnp.exp(s0 - top0[0:1, :]) * (0.5 / z)
            e2_ref[g, rows, :] = jnp.exp(s1 - top1[0:1, :])
            vb = top1[0:PEER_TOPK, :]
            theta = jnp.full((PEER_N_KEYS, LANES), jnp.inf, F32)
            for a in range(PEER_TOPK):
                va = top0[a:a + 1, :]
                th_a = jnp.min(jnp.where(va + vb >= tau, vb, jnp.inf), axis=0, keepdims=True)
                match = (s0 == va) if ranks is None else (ranks[0] == float(a))
                theta = jnp.where(match, th_a, theta)
            th_ref[g, rows, :] = theta
            tied = jnp.where(sums[PEER_TOPK] == tau, 1.0, 0.0)
            tau_ref[g, slab, :] = jnp.broadcast_to(tau, (SUBLANES, LANES))
            ct_ref[g, slab, :] = jnp.broadcast_to(cids[PEER_TOPK - 1], (SUBLANES, LANES))
            flag_ref[g, slab, :] = jnp.broadcast_to(tied, (SUBLANES, LANES))
            return tied

        redo = []
        for g in range(tile // LANES):
            lanes = slice(g * LANES, (g + 1) * LANES)
            scores = []
            repeated = jnp.zeros((1, LANES), F32)
            for c in range(2):
                hc = h * 2 + c
                q = qt_ref[pl.ds(pl.multiple_of(hc * PEER_HALF, PEER_HALF), PEER_HALF), lanes]
                s = _dot(keys_ref[hc], q)
                vals, rep = _extract_values(s, PEER_TOPK + 1)
                for r, v in enumerate(vals):
                    top_ref[g, c, r:r + 1, :] = v
                repeated = jnp.maximum(repeated, rep)
                scores.append(s)
            s0, s1 = scores
            sa_ref[g, rows, :] = s0
            sb_ref[g, rows, :] = s1
            ra_ref[g, rows, :] = jnp.full((PEER_N_KEYS, LANES), NOT_RANKED, F32)
            rb_ref[g, rows, :] = jnp.full((PEER_N_KEYS, LANES), NOT_RANKED, rb_ref.dtype)
            tied = select(g, s0, s1, None)
            redo.append(jnp.max(jnp.maximum(repeated, tied)) > 0.0)

        def exact_group(g):
            ranks = []
            for c, sc in enumerate((sa_ref[g, rows, :], sb_ref[g, rows, :])):
                vals, _, rank = _extract_top(sc, PEER_TOPK + 1, PEER_TOPK, key_id, float(PEER_N_KEYS))
                for r, v in enumerate(vals):
                    top_ref[g, c, r:r + 1, :] = v
                ranks.append(rank)
            ra_ref[g, rows, :] = ranks[0] * float(PEER_TOPK)
            rb_ref[g, rows, :] = ranks[1].astype(rb_ref.dtype)
            select(g, sa_ref[g, rows, :], sb_ref[g, rows, :], ranks)

        for g in range(tile // LANES):
            pl.when(redo[g])(functools.partial(exact_group, g))
        return carry

    lax.fori_loop(0, PEER_HEADS, head, 0)


def _peer_route(x2d, wqt, keys, cand_ids):
    n = x2d.shape[0]
    tile = min(ROUTE_TILE, n)
    segs, _ = _candidate_tables()
    hk = PEER_HEADS * PEER_N_KEYS
    full = lambda a: pl.BlockSpec(a.shape, lambda i: (0,) * a.ndim)
    groups = tile // LANES
    slab = lambda r: pl.BlockSpec((groups, r, LANES), lambda i: (i, 0, 0))
    slab_shape = lambda r: (n // LANES, r, LANES)
    hs = PEER_HEADS * SUBLANES
    outs = ([((D_MODEL, n), BF16)] + [(slab_shape(hk), F32)] * 6 + [(slab_shape(hk), BF16)]
            + [(slab_shape(hs), F32)] * 3)
    out_specs = ([pl.BlockSpec((D_MODEL, tile), lambda i: (0, i))] + [slab(hk)] * 7 + [slab(hs)] * 3)
    return pl.pallas_call(
        functools.partial(_route_kernel, tile=tile, segs=segs),
        grid=(n // tile,),
        in_specs=[pl.BlockSpec((tile, D_MODEL), lambda i: (i, 0)), full(wqt), full(keys), full(cand_ids)],
        out_specs=out_specs,
        out_shape=[jax.ShapeDtypeStruct(s, d) for s, d in outs],
        scratch_shapes=[pltpu.VMEM((2 * PEER_HEADS * PEER_HALF, tile), BF16),
                        pltpu.VMEM((groups, 2, 24, LANES), F32),
                        pltpu.VMEM((groups, N_CAND, LANES), F32)],
        compiler_params=_params(("parallel",)),
        name="peer_route",
    )(x2d, wqt, keys, cand_ids)


def _twice_gelu(h):
    return h * (1.0 + lax.erf(h * (1.0 / math.sqrt(2.0))))


def _dense_kernel(xt_ref, u_ref, vt_ref, sa_ref, sb_ref, e1_ref, e2_ref, th_ref, ra_ref, rb_ref, tau_ref, ct_ref,
                  flag_ref, x_ref, g_ref, b_ref, y_ref, acc_ref, h_ref, w_ref, tied_ref, *, tile, etile):
    s = pl.program_id(1)
    n_tiles = pl.num_programs(1) - 1

    @pl.when(s == 0)
    def _():
        acc_ref[...] = jnp.zeros_like(acc_ref)
        tied_ref[0] = (jnp.max(flag_ref[...]) > 0.0).astype(jnp.int32)
        h_ref[0] = _dot(u_ref[...], xt_ref[...])

    tied = tied_ref[0] > 0

    def step(parity, exact):
        cur, nxt = 1 - parity, parity
        h_ref[nxt] = _dot(u_ref[...], xt_ref[...])
        down = None
        for r in range(etile // PEER_N_KEYS):
            i0 = (s - 1) * (etile // PEER_N_KEYS) + r
            for g in range(tile // LANES):
                lanes = slice(g * LANES, (g + 1) * LANES)
                row_of = lambda ref, h: jnp.broadcast_to(ref[g, pl.ds(h * PEER_N_KEYS + i0, 1), :], (SUBLANES, LANES))
                head_of = lambda ref, h: ref[g, h * SUBLANES:(h + 1) * SUBLANES, :]
                e1_b = [row_of(e1_ref, h) for h in range(PEER_HEADS)]
                if exact:
                    a_b = [row_of(sa_ref, h) for h in range(PEER_HEADS)]
                    ra_b = [row_of(ra_ref, h) for h in range(PEER_HEADS)]
                    tau_b = [head_of(tau_ref, h) for h in range(PEER_HEADS)]
                    ct_b = [head_of(ct_ref, h) for h in range(PEER_HEADS)]
                else:
                    th_b = [row_of(th_ref, h) for h in range(PEER_HEADS)]
                for jc in range(PEER_N_KEYS // GATE_ROWS):
                    if exact:
                        rb = [rb_ref[g, h * PEER_N_KEYS + jc * GATE_ROWS:h * PEER_N_KEYS + (jc + 1) * GATE_ROWS, :]
                              .astype(F32) for h in range(PEER_HEADS)]
                    parts = []
                    for sub in range(GATE_ROWS // SUBLANES):
                        j0 = jc * GATE_ROWS + sub * SUBLANES
                        gate = None
                        for h in range(PEER_HEADS):
                            rows = slice(h * PEER_N_KEYS + j0, h * PEER_N_KEYS + j0 + SUBLANES)
                            if exact:
                                total = a_b[h] + sb_ref[g, rows, :]
                                cand = ra_b[h] + rb[h][sub * SUBLANES:(sub + 1) * SUBLANES]
                                keep = (total > tau_b[h]) | ((total == tau_b[h]) & (cand <= ct_b[h]))
                            else:
                                keep = sb_ref[g, rows, :] >= th_b[h]
                            term = jnp.where(keep, e1_b[h] * e2_ref[g, rows, :], 0.0)
                            gate = term if gate is None else gate + term
                        hv = h_ref[cur, r * PEER_N_KEYS + j0:r * PEER_N_KEYS + j0 + SUBLANES, lanes]
                        parts.append(gate * _twice_gelu(hv))
                    rows = slice(r * PEER_N_KEYS + jc * GATE_ROWS, r * PEER_N_KEYS + (jc + 1) * GATE_ROWS)
                    w_ref[rows, lanes] = jnp.concatenate(parts, axis=0).astype(BF16)
            block = slice(r * PEER_N_KEYS, (r + 1) * PEER_N_KEYS)
            part = _dot(vt_ref[:, block], w_ref[block, :])
            down = part if down is None else down + part
        acc_ref[...] += down

    for parity in range(2):
        for exact in (False, True):
            cond = jnp.logical_and(jnp.logical_and(s >= 1, lax.rem(s, 2) == parity), tied == exact)
            pl.when(cond)(functools.partial(step, parity, exact))

    @pl.when(s == n_tiles)
    def _():
        y = DEEPNORM_ALPHA * x_ref[...] + acc_ref[...].T
        y_ref[...] = _layer_norm(y, g_ref[...], b_ref[...])


def _peer_dense(xt, u, vt, route, x2d, g, b):
    n = x2d.shape[0]
    tile = min(DENSE_TOKEN_TILE, n)
    etile = DENSE_EXPERT_TILE
    steps = PEER_N_EXPERTS // etile
    hk = PEER_HEADS * PEER_N_KEYS
    slab = lambda r: pl.BlockSpec((tile // LANES, r, LANES), lambda i, e: (i, 0, 0))
    full = lambda a: pl.BlockSpec(a.shape, lambda i, e: (0,) * a.ndim)
    return pl.pallas_call(
        functools.partial(_dense_kernel, tile=tile, etile=etile),
        grid=(n // tile, steps + 1),
        in_specs=[pl.BlockSpec((D_MODEL, tile), lambda i, e: (0, i)),
                  pl.BlockSpec((etile, D_MODEL), lambda i, e: (jnp.minimum(e, steps - 1), 0)),
                  pl.BlockSpec((D_MODEL, etile), lambda i, e: (0, jnp.maximum(e - 1, 0)))]
                 + [slab(hk)] * 7 + [slab(PEER_HEADS * SUBLANES)] * 3
                 + [pl.BlockSpec((tile, D_MODEL), lambda i, e: (i, 0)), full(g), full(b)],
        out_specs=pl.BlockSpec((tile, D_MODEL), lambda i, e: (i, 0)),
        out_shape=jax.ShapeDtypeStruct((n, D_MODEL), F32),
        scratch_shapes=[pltpu.VMEM((D_MODEL, tile), F32), pltpu.VMEM((2, etile, tile), F32),
                        pltpu.VMEM((etile, tile), BF16), pltpu.SMEM((1,), jnp.int32)],
        compiler_params=_params(("parallel", "arbitrary")),
        name="peer_dense",
    )(xt, u, vt, *route, x2d, g, b)


def _rope_tables(seq_len):
    pos = jnp.arange(seq_len, dtype=F32)
    inv_freq = ROPE_THETA ** (-jnp.arange(0, MLA_ROPE_DIM, 2, dtype=F32) / MLA_ROPE_DIM)
    ang = pos[:, None] * inv_freq[None, :]
    cos, sin = jnp.cos(ang), jnp.sin(ang)
    ones = jnp.ones((seq_len, MLA_NOPE_DIM), F32)
    pad = jnp.zeros((seq_len, LANES - MLA_QK_DIM), F32)
    cos_tab = jnp.concatenate([ones, cos, cos, pad], axis=1)
    sin_tab = jnp.concatenate([0.0 * ones, -sin, sin, pad], axis=1)
    return cos_tab, sin_tab


def _rotate_half_columns(w):
    half = MLA_ROPE_DIM // 2
    return jnp.concatenate([w[..., half:], w[..., :half]], axis=-1)


def _mixer_weights(w_in, w_uq, w_ukv):
    sb_scale = 1.0 / math.sqrt(SB_HEAD_DIM)
    wsb = jnp.concatenate([w_in[:, :D_SB] * sb_scale, w_in[:, D_SB:3 * D_SB]], axis=1).astype(BF16)
    wc = w_in[:, 3 * D_SB:3 * D_SB + MLA_Q_RANK + MLA_KV_RANK].astype(BF16)
    w_kr = w_in[:, 3 * D_SB + MLA_Q_RANK + MLA_KV_RANK:]
    lead = jnp.zeros((D_MODEL, MLA_NOPE_DIM), F32)
    trail = jnp.zeros((D_MODEL, LANES - MLA_QK_DIM), F32)
    wkr = jnp.concatenate([lead, w_kr, trail, lead, _rotate_half_columns(w_kr), trail], axis=1).astype(BF16)

    wq = w_uq.reshape(MLA_Q_RANK, MLA_HEADS, MLA_QK_DIM)
    qpad = jnp.zeros((MLA_Q_RANK, MLA_HEADS, LANES - MLA_QK_DIM), F32)
    q_plain = jnp.concatenate([wq, qpad], axis=2)
    q_rot = jnp.concatenate([0.0 * wq[..., :MLA_NOPE_DIM], _rotate_half_columns(wq[..., MLA_NOPE_DIM:]), qpad], axis=2)
    wq2 = jnp.concatenate([q_plain.reshape(MLA_Q_RANK, -1), q_rot.reshape(MLA_Q_RANK, -1)], axis=1).astype(BF16)

    wkv = w_ukv.reshape(MLA_KV_RANK, MLA_HEADS, MLA_NOPE_DIM + MLA_V_DIM)
    kpad = jnp.zeros((MLA_KV_RANK, MLA_HEADS, LANES - MLA_NOPE_DIM), F32)
    wk = jnp.concatenate([wkv[..., :MLA_NOPE_DIM], kpad], axis=2).reshape(MLA_KV_RANK, -1)
    wv = wkv[..., MLA_NOPE_DIM:].reshape(MLA_KV_RANK, -1)
    wkv2 = jnp.concatenate([wk, wv], axis=1).astype(BF16)
    return wsb, wc, wkr, wq2, wkv2


def kernel(x, w_in, g_cq, w_uq, g_ckv, w_ukv, w_o, ln1_g, ln1_b, peer_wq, peer_keys, peer_u, peer_v, ln2_g, ln2_b):
    b, s, d = x.shape
    n = b * s
    cos_tab, sin_tab = _rope_tables(s)
    tile = min(ATTN_TILE, s)
    strict_upper = (jnp.arange(tile)[:, None] > jnp.arange(tile)[None, :]).astype(BF16)
    _, cand_ids = _candidate_tables()
    cand_ids = jnp.asarray(np.broadcast_to(cand_ids[:, None], (N_CAND, LANES)).copy())

    x2d = x.reshape(n, d)
    for l in range(DEPTH):
        wsb, wc, wkr, wq2, wkv2 = _mixer_weights(w_in[l], w_uq[l], w_ukv[l])
        sbq, sbk, sbv, mq, mk, mvt = _in_proj(x2d, s, wsb, wc, wkr, g_cq[l][None, :], g_ckv[l][None, :], wq2, wkv2,
                                              cos_tab, sin_tab)
        sh = lambda a: a.reshape(b, s, a.shape[-1])
        o_sb = _sb_attention(sh(sbq), sh(sbk), sh(sbv), strict_upper)
        o_mla = _mla_attention(sh(mq), sh(mk), mvt)
        x2d = _out_proj_ln(o_sb.reshape(n, D_SB), o_mla.reshape(n, D_MLA_OUT), x2d, w_o[l].astype(BF16),
                           ln1_g[l][None, :], ln1_b[l][None, :])
        wqt = peer_wq[l].T.astype(BF16)
        keys = peer_keys[l].reshape(PEER_HEADS * 2, PEER_N_KEYS, PEER_HALF).astype(BF16)
        route = _peer_route(x2d, wqt, keys, cand_ids)
        x2d = _peer_dense(route[0], peer_u[l].astype(BF16), peer_v[l].T.astype(BF16), route[1:], x2d,
                          ln2_g[l][None, :], ln2_b[l][None, :])
    return x2d.reshape(b, s, d)
```

```python
import functools
import math

import jax
import jax.numpy as jnp
import numpy as np
from jax import lax
from jax.experimental import pallas as pl
from jax.experimental.pallas import tpu as pltpu

D_MODEL = 1024
DEPTH = 2
SB_HEADS = 8
SB_HEAD_DIM = 64
D_SB = SB_HEADS * SB_HEAD_DIM
MLA_HEADS = 8
MLA_NOPE_DIM = 64
MLA_ROPE_DIM = 32
MLA_QK_DIM = MLA_NOPE_DIM + MLA_ROPE_DIM
MLA_V_DIM = 64
MLA_Q_RANK = 256
MLA_KV_RANK = 128
D_MLA_OUT = MLA_HEADS * MLA_V_DIM
ROPE_THETA = 10000.0
PEER_HEADS = 8
PEER_N_KEYS = 128
PEER_N_EXPERTS = PEER_N_KEYS * PEER_N_KEYS
PEER_HALF = 128
PEER_TOPK = 16
DEEPNORM_ALPHA = (2.0 * DEPTH) ** 0.25
LN_EPS = 1e-5
RMS_EPS = 1e-6

LANES = 128
SUBLANES = 8
VMEM_LIMIT_BYTES = 56 * 1024 * 1024

TOKEN_TILE = 512
ATTN_TILE = 256
SB_PAIRS_PER_STEP = 1
MLA_TILE = 1024
ROUTE_TILE = 512
DENSE_TOKEN_TILE = 256
DENSE_EXPERT_TILE = 2048
GATE_ROWS = 16

EXP_UNDERFLOW = -120.0
NOT_RANKED = 4096.0
N_CAND = 56

F32 = jnp.float32
BF16 = jnp.bfloat16


def _dot(a, b):
    return jnp.dot(a, b, preferred_element_type=F32)


def _dot_nt(a, b):
    return lax.dot_general(a, b, (((1,), (1,)), ((), ())), preferred_element_type=F32)


def _params(semantics, flags=None):
    return pltpu.CompilerParams(dimension_semantics=semantics, vmem_limit_bytes=VMEM_LIMIT_BYTES, flags=flags)


def _in_proj_kernel(x_ref, wsb_ref, wc_ref, wkr_ref, gq_ref, gkv_ref, wq2_ref, wkv_ref, cos_ref, sin_ref,
                    sbq_ref, sbk_ref, sbv_ref, mq_ref, mk_ref, mv_ref):
    xb = x_ref[...].astype(BF16)
    sb = _dot(xb, wsb_ref[...])
    sbq_ref[...] = sb[:, :D_SB].astype(BF16)
    sbk_ref[...] = sb[:, D_SB:2 * D_SB].astype(BF16)
    sbv_ref[...] = sb[:, 2 * D_SB:].astype(BF16)

    c = _dot(xb, wc_ref[...])
    cq = c[:, :MLA_Q_RANK]
    ckv = c[:, MLA_Q_RANK:]
    nq = cq * lax.rsqrt(jnp.mean(cq * cq, axis=-1, keepdims=True) + RMS_EPS) * gq_ref[...]
    nkv = ckv * lax.rsqrt(jnp.mean(ckv * ckv, axis=-1, keepdims=True) + RMS_EPS) * gkv_ref[...]

    cos = cos_ref[...]
    sin = sin_ref[...]
    kr = _dot(xb, wkr_ref[...])
    k_rope = kr[:, :LANES] * cos + kr[:, LANES:] * sin

    q2 = _dot(nq.astype(BF16), wq2_ref[...])
    kv = _dot(nkv.astype(BF16), wkv_ref[...])
    scale = 1.0 / math.sqrt(MLA_QK_DIM)
    for h in range(MLA_HEADS):
        lo, hi = h * LANES, (h + 1) * LANES
        qa = q2[:, lo:hi]
        qb = q2[:, MLA_HEADS * LANES + lo:MLA_HEADS * LANES + hi]
        mq_ref[:, lo:hi] = ((qa * cos + qb * sin) * scale).astype(BF16)
        mk_ref[:, lo:hi] = (kv[:, lo:hi] + k_rope).astype(BF16)
    mv_ref[...] = kv[:, MLA_HEADS * LANES:].T.astype(BF16)


def _in_proj(x2d, seq_len, wsb, wc, wkr, gq, gkv, wq2, wkv, cos_tab, sin_tab):
    n = x2d.shape[0]
    tm = min(TOKEN_TILE, seq_len)
    pos_tiles = seq_len // tm
    full = lambda a: pl.BlockSpec(a.shape, lambda i: (0,) * a.ndim)
    row = lambda w: pl.BlockSpec((tm, w), lambda i: (i, 0))
    col = lambda r: pl.BlockSpec((r, tm), lambda i: (0, i))
    pos = pl.BlockSpec((tm, LANES), lambda i: (i % pos_tiles, 0))
    wide = MLA_HEADS * LANES
    outs = [(n, D_SB), (n, D_SB), (n, D_SB), (n, wide), (n, wide), (D_MLA_OUT, n)]
    out_specs = [row(D_SB), row(D_SB), row(D_SB), row(wide), row(wide), col(D_MLA_OUT)]
    return pl.pallas_call(
        _in_proj_kernel,
        grid=(n // tm,),
        in_specs=[row(D_MODEL), full(wsb), full(wc), full(wkr), full(gq), full(gkv), full(wq2), full(wkv), pos, pos],
        out_specs=out_specs,
        out_shape=[jax.ShapeDtypeStruct(s, BF16) for s in outs],
        compiler_params=_params(("parallel",)),
        name="in_proj",
    )(x2d, wsb, wc, wkr, gq, gkv, wq2, wkv, cos_tab, sin_tab)


def _sb_attn_kernel(q_ref, k_ref, v_ref, u_ref, o_ref, acc_ref, c_ref, *, tile, pairs):
    qi = pl.program_id(2)
    lane = lax.broadcasted_iota(jnp.int32, (tile, LANES), 1)
    q_heads = []
    for p in range(pairs):
        q2 = q_ref[0, :, p * LANES:(p + 1) * LANES]
        zero = jnp.zeros_like(q2)
        q_heads += [jnp.where(lane < SB_HEAD_DIM, q2, zero), jnp.where(lane >= SB_HEAD_DIM, q2, zero)]
    u = u_ref[...]
    r_id = lax.broadcasted_iota(jnp.int32, (tile, tile), 0)
    c_id = lax.broadcasted_iota(jnp.int32, (tile, tile), 1)
    causal = c_id < r_id

    def sticks(hh, j, mask):
        start = pl.multiple_of(j * tile, tile)
        pair = slice((hh // 2) * LANES, (hh // 2 + 1) * LANES)
        z = _dot_nt(q_heads[hh], k_ref[0, pl.ds(start, tile), pair])
        softplus = jnp.maximum(z, 0.0) + jnp.log1p(jnp.exp(-jnp.abs(z)))
        log_1m = -softplus
        if mask is not None:
            log_1m = jnp.where(mask, log_1m, 0.0)
        hi = log_1m.astype(BF16)
        lo = (log_1m - hi.astype(F32)).astype(BF16)
        return z - softplus, _dot(hi, u) + _dot(lo, u), jnp.sum(log_1m, axis=1, keepdims=True)

    def weighted_values(hh, j, w):
        start = pl.multiple_of(j * tile, tile)
        pair = slice((hh // 2) * LANES, (hh // 2 + 1) * LANES)
        return _dot(w.astype(BF16), v_ref[0, pl.ds(start, tile), pair])

    has_left = qi >= 1
    left = jnp.maximum(qi - 1, 0)
    everywhere = jnp.broadcast_to(has_left, (tile, tile))
    for hh in range(2 * pairs):
        log_beta_d, tail_d, sum_d = sticks(hh, qi, causal)
        log_beta_l, tail_l, sum_l = sticks(hh, left, everywhere)
        w_d = jnp.where(causal, jnp.exp(log_beta_d + tail_d), 0.0)
        w_l = jnp.where(everywhere, jnp.exp(log_beta_l + tail_l + sum_d), 0.0)
        acc_ref[hh] = weighted_values(hh, qi, w_d) + weighted_values(hh, left, w_l)
        c_ref[hh] = sum_d + sum_l

    def block(j):
        for hh in range(2 * pairs):
            log_beta, tail, row_sum = sticks(hh, j, None)
            w = jnp.exp(log_beta + tail + c_ref[hh])
            acc_ref[hh] += weighted_values(hh, j, w)
            c_ref[hh] += row_sum

    def more(carry):
        t, c_max = carry
        return jnp.logical_and(t < qi, c_max > EXP_UNDERFLOW)

    def body(carry):
        t, _ = carry
        block(qi - 1 - t)
        return t + 1, jnp.max(c_ref[...])

    lax.while_loop(more, body, (jnp.int32(1), jnp.max(c_ref[...])))
    for p in range(pairs):
        o_pair = jnp.where(lane < SB_HEAD_DIM, acc_ref[2 * p], acc_ref[2 * p + 1])
        o_ref[0, :, p * LANES:(p + 1) * LANES] = o_pair.astype(o_ref.dtype)


def _sb_attention(q, k, v, u):
    b, s, _ = q.shape
    tile = min(ATTN_TILE, s)
    pairs = SB_PAIRS_PER_STEP
    width = pairs * LANES
    q_spec = pl.BlockSpec((1, tile, width), lambda bi, p, i: (bi, i, p))
    kv_spec = pl.BlockSpec((1, s, width), lambda bi, p, i: (bi, 0, p))
    return pl.pallas_call(
        functools.partial(_sb_attn_kernel, tile=tile, pairs=pairs),
        grid=(b, D_SB // width, s // tile),
        in_specs=[q_spec, kv_spec, kv_spec, pl.BlockSpec((tile, tile), lambda bi, p, i: (0, 0))],
        out_specs=q_spec,
        out_shape=jax.ShapeDtypeStruct((b, s, D_SB), BF16),
        scratch_shapes=[pltpu.VMEM((2 * pairs, tile, LANES), F32), pltpu.VMEM((2 * pairs, tile, 1), F32)],
        compiler_params=_params(("parallel", "parallel", "arbitrary")),
        name="sb_attn",
    )(q, k, v, u)


def _mla_attn_kernel(q_ref, k_ref, vt_ref, o_ref, m_ref, l_ref, acc_ref, *, tile):
    qi = pl.program_id(2)
    q2 = q_ref[0]
    k_id = lax.broadcasted_iota(jnp.int32, (tile, tile), 0)
    q_id = lax.broadcasted_iota(jnp.int32, (tile, tile), 1)
    causal = k_id <= q_id

    m_ref[...] = jnp.full_like(m_ref, -jnp.inf)
    l_ref[...] = jnp.zeros_like(l_ref)
    acc_ref[...] = jnp.zeros_like(acc_ref)

    def block(j, masked):
        start = pl.multiple_of(j * tile, tile)
        kb = k_ref[0, pl.ds(start, tile), :]
        vtb = vt_ref[:, pl.ds(start, tile)]
        for hh in range(2):
            s = _dot_nt(kb[:, hh * LANES:(hh + 1) * LANES], q2[:, hh * LANES:(hh + 1) * LANES])
            if masked:
                s = jnp.where(causal, s, -jnp.inf)
            m_old = m_ref[hh]
            m_new = jnp.maximum(m_old, jnp.max(s, axis=0, keepdims=True))
            p = jnp.exp(s - m_new)
            alpha = jnp.exp(m_old - m_new)
            l_ref[hh] = alpha * l_ref[hh] + jnp.sum(p, axis=0, keepdims=True)
            pv = _dot(vtb[hh * MLA_V_DIM:(hh + 1) * MLA_V_DIM, :], p.astype(BF16))
            acc_ref[hh] = alpha * acc_ref[hh] + pv
            m_ref[hh] = m_new

    block(qi, True)

    def body(j, carry):
        block(j, False)
        return carry

    lax.fori_loop(0, qi, body, 0)
    o_t = jnp.concatenate([acc_ref[0] / l_ref[0], acc_ref[1] / l_ref[1]], axis=0)
    o_ref[0] = o_t.T.astype(o_ref.dtype)


def _mla_attention(q, k, vt):
    b, s, _ = q.shape
    tile = min(MLA_TILE, s)
    pairs = MLA_HEADS // 2
    q_spec = pl.BlockSpec((1, tile, 2 * LANES), lambda bi, p, i: (bi, i, p))
    k_spec = pl.BlockSpec((1, s, 2 * LANES), lambda bi, p, i: (bi, 0, p))
    vt_spec = pl.BlockSpec((LANES, s), lambda bi, p, i: (p, bi))
    o_spec = pl.BlockSpec((1, tile, LANES), lambda bi, p, i: (bi, i, p))
    return pl.pallas_call(
        functools.partial(_mla_attn_kernel, tile=tile),
        grid=(b, pairs, s // tile),
        in_specs=[q_spec, k_spec, vt_spec],
        out_specs=o_spec,
        out_shape=jax.ShapeDtypeStruct((b, s, D_MLA_OUT), BF16),
        scratch_shapes=[pltpu.VMEM((2, 1, tile), F32), pltpu.VMEM((2, 1, tile), F32),
                        pltpu.VMEM((2, MLA_V_DIM, tile), F32)],
        compiler_params=_params(("parallel", "parallel", "arbitrary")),
        name="mla_attn",
    )(q, k, vt)


def _layer_norm(y, g, b):
    mu = jnp.mean(y, axis=-1, keepdims=True)
    d = y - mu
    var = jnp.mean(d * d, axis=-1, keepdims=True)
    return d * lax.rsqrt(var + LN_EPS) * g + b


def _out_proj_kernel(osb_ref, omla_ref, x_ref, wo_ref, g_ref, b_ref, y_ref):
    mix = _dot(osb_ref[...], wo_ref[:D_SB, :]) + _dot(omla_ref[...], wo_ref[D_SB:, :])
    y_ref[...] = _layer_norm(DEEPNORM_ALPHA * x_ref[...] + mix, g_ref[...], b_ref[...])


def _out_proj_ln(osb, omla, x2d, wo, g, b):
    n = x2d.shape[0]
    tm = min(TOKEN_TILE, n)
    full = lambda a: pl.BlockSpec(a.shape, lambda i: (0,) * a.ndim)
    row = lambda w: pl.BlockSpec((tm, w), lambda i: (i, 0))
    return pl.pallas_call(
        _out_proj_kernel,
        grid=(n // tm,),
        in_specs=[row(D_SB), row(D_MLA_OUT), row(D_MODEL), full(wo), full(g), full(b)],
        out_specs=row(D_MODEL),
        out_shape=jax.ShapeDtypeStruct((n, D_MODEL), F32),
        compiler_params=_params(("parallel",)),
        name="out_proj_ln",
    )(osb, omla, x2d, wo, g, b)


def _candidate_tables():
    segs = [(a, PEER_TOPK // (a + 1)) for a in range(8)]
    ids = []
    for a, nb in segs:
        ids += [a * PEER_TOPK + b for b in range(nb)]
    ids += [a * PEER_TOPK for a in range(8, PEER_TOPK)]
    ids += [1000.0, 1001.0]
    ids += [2000.0 + i for i in range(N_CAND - len(ids))]
    return segs, np.asarray(ids, np.float32)


def _extract_top(s, n_take, n_rank, row_id, id_sentinel):
    rank = jnp.full(s.shape, NOT_RANKED, F32)
    vals, ids = [], []
    for r in range(n_take):
        m = jnp.max(s, axis=0, keepdims=True)
        i = jnp.min(jnp.where(s == m, row_id, id_sentinel), axis=0, keepdims=True)
        hit = row_id == i
        s = jnp.where(hit, -jnp.inf, s)
        if r < n_rank:
            rank = jnp.where(hit, float(r), rank)
        vals.append(m)
        ids.append(i)
    return vals, ids, rank


def _extract_values(s, n_take):
    vals = []
    for _ in range(n_take):
        m = jnp.max(s, axis=0, keepdims=True)
        s = jnp.where(s == m, -jnp.inf, s)
        vals.append(m)
    removed = jnp.sum(jnp.where(s == -jnp.inf, 1.0, 0.0), axis=0, keepdims=True)
    return vals, jnp.where(removed == float(n_take), 0.0, 1.0)


def _route_kernel(x_ref, wqt_ref, keys_ref, cid_ref,
                  xt_ref, sa_ref, sb_ref, e1_ref, e2_ref, th_ref, ra_ref, rb_ref, tau_ref, ct_ref, flag_ref,
                  qt_ref, top_ref, cand_ref, *, tile, segs):
    xb = x_ref[...].astype(BF16)
    xt_ref[...] = xb.T
    qt_ref[...] = _dot_nt(wqt_ref[...], xb).astype(BF16)
    key_id = lax.broadcasted_iota(jnp.int32, (PEER_N_KEYS, LANES), 0).astype(F32)
    cand_id = cid_ref[...]

    def head(h, carry):
        rows = pl.ds(pl.multiple_of(h * PEER_N_KEYS, PEER_N_KEYS), PEER_N_KEYS)
        slab = pl.ds(pl.multiple_of(h * SUBLANES, SUBLANES), SUBLANES)

        def select(g, s0, s1, ranks):
            top0, top1, cand = top_ref.at[g, 0], top_ref.at[g, 1], cand_ref.at[g]
            off = 0
            for a, nb in segs:
                cand[off:off + nb, :] = top0[a:a + 1, :] + top1[0:nb, :]
                off += nb
            cand[off:off + 8, :] = top0[8:16, :] + top1[0:1, :]
            off += 8
            cand[off:off + 1, :] = top0[16:17, :] + top1[0:1, :]
            cand[off + 1:off + 2, :] = top0[0:1, :] + top1[16:17, :]
            cand[off + 2:, :] = jnp.full((N_CAND - off - 2, LANES), -jnp.inf, F32)
            sums, cids, _ = _extract_top(cand[...], PEER_TOPK + 1, 0, cand_id, 4000.0)
            z = jnp.zeros_like(sums[0])
            for r in range(PEER_TOPK):
                z = z + jnp.exp(sums[r] - sums[0])
            tau = sums[PEER_TOPK - 1]
            e1_ref[g, rows, :] = jnp.exp(s0 - top0[0:1, :]) * (0.5 / z)
            e2_ref[g, rows, :] = jnp.exp(s1 - top1[0:1, :])
            vb = top1[0:PEER_TOPK, :]
            theta = jnp.full((PEER_N_KEYS, LANES), jnp.inf, F32)
            for a in range(PEER_TOPK):
                va = top0[a:a + 1, :]
                th_a = jnp.min(jnp.where(va + vb >= tau, vb, jnp.inf), axis=0, keepdims=True)
                match = (s0 == va) if ranks is None else (ranks[0] == float(a))
                theta = jnp.where(match, th_a, theta)
            th_ref[g, rows, :] = theta
            tied = jnp.where(sums[PEER_TOPK] == tau, 1.0, 0.0)
            tau_ref[g, slab, :] = jnp.broadcast_to(tau, (SUBLANES, LANES))
            ct_ref[g, slab, :] = jnp.broadcast_to(cids[PEER_TOPK - 1], (SUBLANES, LANES))
            flag_ref[g, slab, :] = jnp.broadcast_to(tied, (SUBLANES, LANES))
            return tied

        redo = []
        for g in range(tile // LANES):
            lanes = slice(g * LANES, (g + 1) * LANES)
            scores = []
            repeated = jnp.zeros((1, LANES), F32)
            for c in range(2):
                hc = h * 2 + c
                q = qt_ref[pl.ds(pl.multiple_of(hc * PEER_HALF, PEER_HALF), PEER_HALF), lanes]
                s = _dot(keys_ref[hc], q)
                vals, rep = _extract_values(s, PEER_TOPK + 1)
                for r, v in enumerate(vals):
                    top_ref[g, c, r:r + 1, :] = v
                repeated = jnp.maximum(repeated, rep)
                scores.append(s)
            s0, s1 = scores
            sa_ref[g, rows, :] = s0
            sb_ref[g, rows, :] = s1
            ra_ref[g, rows, :] = jnp.full((PEER_N_KEYS, LANES), NOT_RANKED, F32)
            rb_ref[g, rows, :] = jnp.full((PEER_N_KEYS, LANES), NOT_RANKED, rb_ref.dtype)
            tied = select(g, s0, s1, None)
            redo.append(jnp.max(jnp.maximum(repeated, tied)) > 0.0)

        def exact_group(g):
            ranks = []
            for c, sc in enumerate((sa_ref[g, rows, :], sb_ref[g, rows, :])):
                vals, _, rank = _extract_top(sc, PEER_TOPK + 1, PEER_TOPK, key_id, float(PEER_N_KEYS))
                for r, v in enumerate(vals):
                    top_ref[g, c, r:r + 1, :] = v
                ranks.append(rank)
            ra_ref[g, rows, :] = ranks[0] * float(PEER_TOPK)
            rb_ref[g, rows, :] = ranks[1].astype(rb_ref.dtype)
            select(g, sa_ref[g, rows, :], sb_ref[g, rows, :], ranks)

        for g in range(tile // LANES):
            pl.when(redo[g])(functools.partial(exact_group, g))
        return carry

    lax.fori_loop(0, PEER_HEADS, head, 0)


def _peer_route(x2d, wqt, keys, cand_ids):
    n = x2d.shape[0]
    tile = min(ROUTE_TILE, n)
    segs, _ = _candidate_tables()
    hk = PEER_HEADS * PEER_N_KEYS
    full = lambda a: pl.BlockSpec(a.shape, lambda i: (0,) * a.ndim)
    groups = tile // LANES
    slab = lambda r: pl.BlockSpec((groups, r, LANES), lambda i: (i, 0, 0))
    slab_shape = lambda r: (n // LANES, r, LANES)
    hs = PEER_HEADS * SUBLANES
    outs = ([((D_MODEL, n), BF16)] + [(slab_shape(hk), F32)] * 6 + [(slab_shape(hk), BF16)]
            + [(slab_shape(hs), F32)] * 3)
    out_specs = ([pl.BlockSpec((D_MODEL, tile), lambda i: (0, i))] + [slab(hk)] * 7 + [slab(hs)] * 3)
    return pl.pallas_call(
        functools.partial(_route_kernel, tile=tile, segs=segs),
        grid=(n // tile,),
        in_specs=[pl.BlockSpec((tile, D_MODEL), lambda i: (i, 0)), full(wqt), full(keys), full(cand_ids)],
        out_specs=out_specs,
        out_shape=[jax.ShapeDtypeStruct(s, d) for s, d in outs],
        scratch_shapes=[pltpu.VMEM((2 * PEER_HEADS * PEER_HALF, tile), BF16),
                        pltpu.VMEM((groups, 2, 24, LANES), F32),
                        pltpu.VMEM((groups, N_CAND, LANES), F32)],
        compiler_params=_params(("parallel",)),
        name="peer_route",
    )(x2d, wqt, keys, cand_ids)


def _twice_gelu(h):
    return h * (1.0 + lax.erf(h * (1.0 / math.sqrt(2.0))))


def _dense_kernel(xt_ref, u_ref, vt_ref, sa_ref, sb_ref, e1_ref, e2_ref, th_ref, ra_ref, rb_ref, tau_ref, ct_ref,
                  flag_ref, x_ref, g_ref, b_ref, y_ref, acc_ref, h_ref, w_ref, tied_ref, *, tile, etile):
    s = pl.program_id(1)
    n_tiles = pl.num_programs(1) - 1

    @pl.when(s == 0)
    def _():
        acc_ref[...] = jnp.zeros_like(acc_ref)
        tied_ref[0] = (jnp.max(flag_ref[...]) > 0.0).astype(jnp.int32)
        h_ref[0] = _dot(u_ref[...], xt_ref[...])

    tied = tied_ref[0] > 0

    def step(parity, exact):
        cur, nxt = 1 - parity, parity
        h_ref[nxt] = _dot(u_ref[...], xt_ref[...])
        down = None
        for r in range(etile // PEER_N_KEYS):
            i0 = (s - 1) * (etile // PEER_N_KEYS) + r
            for g in range(tile // LANES):
                lanes = slice(g * LANES, (g + 1) * LANES)
                row_of = lambda ref, h: jnp.broadcast_to(ref[g, pl.ds(h * PEER_N_KEYS + i0, 1), :], (SUBLANES, LANES))
                head_of = lambda ref, h: ref[g, h * SUBLANES:(h + 1) * SUBLANES, :]
                e1_b = [row_of(e1_ref, h) for h in range(PEER_HEADS)]
                if exact:
                    a_b = [row_of(sa_ref, h) for h in range(PEER_HEADS)]
                    ra_b = [row_of(ra_ref, h) for h in range(PEER_HEADS)]
                    tau_b = [head_of(tau_ref, h) for h in range(PEER_HEADS)]
                    ct_b = [head_of(ct_ref, h) for h in range(PEER_HEADS)]
                else:
                    th_b = [row_of(th_ref, h) for h in range(PEER_HEADS)]
                for jc in range(PEER_N_KEYS // GATE_ROWS):
                    if exact:
                        rb = [rb_ref[g, h * PEER_N_KEYS + jc * GATE_ROWS:h * PEER_N_KEYS + (jc + 1) * GATE_ROWS, :]
                              .astype(F32) for h in range(PEER_HEADS)]
                    parts = []
                    for sub in range(GATE_ROWS // SUBLANES):
                        j0 = jc * GATE_ROWS + sub * SUBLANES
                        gate = None
                        for h in range(PEER_HEADS):
                            rows = slice(h * PEER_N_KEYS + j0, h * PEER_N_KEYS + j0 + SUBLANES)
                            if exact:
                                total = a_b[h] + sb_ref[g, rows, :]
                                cand = ra_b[h] + rb[h][sub * SUBLANES:(sub + 1) * SUBLANES]
                                keep = (total > tau_b[h]) | ((total == tau_b[h]) & (cand <= ct_b[h]))
                            else:
                                keep = sb_ref[g, rows, :] >= th_b[h]
                            term = jnp.where(keep, e1_b[h] * e2_ref[g, rows, :], 0.0)
                            gate = term if gate is None else gate + term
                        hv = h_ref[cur, r * PEER_N_KEYS + j0:r * PEER_N_KEYS + j0 + SUBLANES, lanes]
                        parts.append(gate * _twice_gelu(hv))
                    rows = slice(r * PEER_N_KEYS + jc * GATE_ROWS, r * PEER_N_KEYS + (jc + 1) * GATE_ROWS)
                    w_ref[rows, lanes] = jnp.concatenate(parts, axis=0).astype(BF16)
            block = slice(r * PEER_N_KEYS, (r + 1) * PEER_N_KEYS)
            part = _dot(vt_ref[:, block], w_ref[block, :])
            down = part if down is None else down + part
        acc_ref[...] += down

    for parity in range(2):
        for exact in (False, True):
            cond = jnp.logical_and(jnp.logical_and(s >= 1, lax.rem(s, 2) == parity), tied == exact)
            pl.when(cond)(functools.partial(step, parity, exact))

    @pl.when(s == n_tiles)
    def _():
        y = DEEPNORM_ALPHA * x_ref[...] + acc_ref[...].T
        y_ref[...] = _layer_norm(y, g_ref[...], b_ref[...])


def _peer_dense(xt, u, vt, route, x2d, g, b):
    n = x2d.shape[0]
    tile = min(DENSE_TOKEN_TILE, n)
    etile = DENSE_EXPERT_TILE
    steps = PEER_N_EXPERTS // etile
    hk = PEER_HEADS * PEER_N_KEYS
    slab = lambda r: pl.BlockSpec((tile // LANES, r, LANES), lambda i, e: (i, 0, 0))
    full = lambda a: pl.BlockSpec(a.shape, lambda i, e: (0,) * a.ndim)
    return pl.pallas_call(
        functools.partial(_dense_kernel, tile=tile, etile=etile),
        grid=(n // tile, steps + 1),
        in_specs=[pl.BlockSpec((D_MODEL, tile), lambda i, e: (0, i)),
                  pl.BlockSpec((etile, D_MODEL), lambda i, e: (jnp.minimum(e, steps - 1), 0)),
                  pl.BlockSpec((D_MODEL, etile), lambda i, e: (0, jnp.maximum(e - 1, 0)))]
                 + [slab(hk)] * 7 + [slab(PEER_HEADS * SUBLANES)] * 3
                 + [pl.BlockSpec((tile, D_MODEL), lambda i, e: (i, 0)), full(g), full(b)],
        out_specs=pl.BlockSpec((tile, D_MODEL), lambda i, e: (i, 0)),
        out_shape=jax.ShapeDtypeStruct((n, D_MODEL), F32),
        scratch_shapes=[pltpu.VMEM((D_MODEL, tile), F32), pltpu.VMEM((2, etile, tile), F32),
                        pltpu.VMEM((etile, tile), BF16), pltpu.SMEM((1,), jnp.int32)],
        compiler_params=_params(("parallel", "arbitrary")),
        name="peer_dense",
    )(xt, u, vt, *route, x2d, g, b)


def _rope_tables(seq_len):
    pos = jnp.arange(seq_len, dtype=F32)
    inv_freq = ROPE_THETA ** (-jnp.arange(0, MLA_ROPE_DIM, 2, dtype=F32) / MLA_ROPE_DIM)
    ang = pos[:, None] * inv_freq[None, :]
    cos, sin = jnp.cos(ang), jnp.sin(ang)
    ones = jnp.ones((seq_len, MLA_NOPE_DIM), F32)
    pad = jnp.zeros((seq_len, LANES - MLA_QK_DIM), F32)
    cos_tab = jnp.concatenate([ones, cos, cos, pad], axis=1)
    sin_tab = jnp.concatenate([0.0 * ones, -sin, sin, pad], axis=1)
    return cos_tab, sin_tab


def _rotate_half_columns(w):
    half = MLA_ROPE_DIM // 2
    return jnp.concatenate([w[..., half:], w[..., :half]], axis=-1)


def _mixer_weights(w_in, w_uq, w_ukv):
    sb_scale = 1.0 / math.sqrt(SB_HEAD_DIM)
    wsb = jnp.concatenate([w_in[:, :D_SB] * sb_scale, w_in[:, D_SB:3 * D_SB]], axis=1).astype(BF16)
    wc = w_in[:, 3 * D_SB:3 * D_SB + MLA_Q_RANK + MLA_KV_RANK].astype(BF16)
    w_kr = w_in[:, 3 * D_SB + MLA_Q_RANK + MLA_KV_RANK:]
    lead = jnp.zeros((D_MODEL, MLA_NOPE_DIM), F32)
    trail = jnp.zeros((D_MODEL, LANES - MLA_QK_DIM), F32)
    wkr = jnp.concatenate([lead, w_kr, trail, lead, _rotate_half_columns(w_kr), trail], axis=1).astype(BF16)

    wq = w_uq.reshape(MLA_Q_RANK, MLA_HEADS, MLA_QK_DIM)
    qpad = jnp.zeros((MLA_Q_RANK, MLA_HEADS, LANES - MLA_QK_DIM), F32)
    q_plain = jnp.concatenate([wq, qpad], axis=2)
    q_rot = jnp.concatenate([0.0 * wq[..., :MLA_NOPE_DIM], _rotate_half_columns(wq[..., MLA_NOPE_DIM:]), qpad], axis=2)
    wq2 = jnp.concatenate([q_plain.reshape(MLA_Q_RANK, -1), q_rot.reshape(MLA_Q_RANK, -1)], axis=1).astype(BF16)

    wkv = w_ukv.reshape(MLA_KV_RANK, MLA_HEADS, MLA_NOPE_DIM + MLA_V_DIM)
    kpad = jnp.zeros((MLA_KV_RANK, MLA_HEADS, LANES - MLA_NOPE_DIM), F32)
    wk = jnp.concatenate([wkv[..., :MLA_NOPE_DIM], kpad], axis=2).reshape(MLA_KV_RANK, -1)
    wv = wkv[..., MLA_NOPE_DIM:].reshape(MLA_KV_RANK, -1)
    wkv2 = jnp.concatenate([wk, wv], axis=1).astype(BF16)
    return wsb, wc, wkr, wq2, wkv2


def kernel(x, w_in, g_cq, w_uq, g_ckv, w_ukv, w_o, ln1_g, ln1_b, peer_wq, peer_keys, peer_u, peer_v, ln2_g, ln2_b):
    b, s, d = x.shape
    n = b * s
    cos_tab, sin_tab = _rope_tables(s)
    tile = min(ATTN_TILE, s)
    strict_upper = (jnp.arange(tile)[:, None] > jnp.arange(tile)[None, :]).astype(BF16)
    _, cand_ids = _candidate_tables()
    cand_ids = jnp.asarray(np.broadcast_to(cand_ids[:, None], (N_CAND, LANES)).copy())

    x2d = x.reshape(n, d)
    for l in range(DEPTH):
        wsb, wc, wkr, wq2, wkv2 = _mixer_weights(w_in[l], w_uq[l], w_ukv[l])
        sbq, sbk, sbv, mq, mk, mvt = _in_proj(x2d, s, wsb, wc, wkr, g_cq[l][None, :], g_ckv[l][None, :], wq2, wkv2,
                                              cos_tab, sin_tab)
        sh = lambda a: a.reshape(b, s, a.shape[-1])
        o_sb = _sb_attention(sh(sbq), sh(sbk), sh(sbv), strict_upper)
        o_mla = _mla_attention(sh(mq), sh(mk), mvt)
        x2d = _out_proj_ln(o_sb.reshape(n, D_SB), o_mla.reshape(n, D_MLA_OUT), x2d, w_o[l].astype(BF16),
                           ln1_g[l][None, :], ln1_b[l][None, :])
        wqt = peer_wq[l].T.astype(BF16)
        keys = peer_keys[l].reshape(PEER_HEADS * 2, PEER_N_KEYS, PEER_HALF).astype(BF16)
        route = _peer_route(x2d, wqt, keys, cand_ids)
        x2d = _peer_dense(route[0], peer_u[l].astype(BF16), peer_v[l].T.astype(BF16), route[1:], x2d,
                          ln2_g[l][None, :], ln2_b[l][None, :])
    return x2d.reshape(b, s, d)
```

```python
import functools
import math

import jax
import jax.numpy as jnp
import numpy as np
from jax import lax
from jax.experimental import pallas as pl
from jax.experimental.pallas import tpu as pltpu

D_MODEL = 1024
DEPTH = 2
SB_HEADS = 8
SB_HEAD_DIM = 64
D_SB = SB_HEADS * SB_HEAD_DIM
MLA_HEADS = 8
MLA_NOPE_DIM = 64
MLA_ROPE_DIM = 32
MLA_QK_DIM = MLA_NOPE_DIM + MLA_ROPE_DIM
MLA_V_DIM = 64
MLA_Q_RANK = 256
MLA_KV_RANK = 128
D_MLA_OUT = MLA_HEADS * MLA_V_DIM
ROPE_THETA = 10000.0
PEER_HEADS = 8
PEER_N_KEYS = 128
PEER_N_EXPERTS = PEER_N_KEYS * PEER_N_KEYS
PEER_HALF = 128
PEER_TOPK = 16
DEEPNORM_ALPHA = (2.0 * DEPTH) ** 0.25
LN_EPS = 1e-5
RMS_EPS = 1e-6

LANES = 128
SUBLANES = 8
VMEM_LIMIT_BYTES = 56 * 1024 * 1024

TOKEN_TILE = 512
ATTN_TILE = 256
SB_PAIRS_PER_STEP = 1
MLA_TILE = 1024
ROUTE_TILE = 512
DENSE_TOKEN_TILE = 256
DENSE_EXPERT_TILE = 2048
GATE_ROWS = 16

EXP_UNDERFLOW = -120.0
NOT_RANKED = 4096.0
N_CAND = 56

F32 = jnp.float32
BF16 = jnp.bfloat16


def _dot(a, b):
    return jnp.dot(a, b, preferred_element_type=F32)


def _dot_nt(a, b):
    return lax.dot_general(a, b, (((1,), (1,)), ((), ())), preferred_element_type=F32)


def _params(semantics, flags=None):
    return pltpu.CompilerParams(dimension_semantics=semantics, vmem_limit_bytes=VMEM_LIMIT_BYTES, flags=flags)


def _in_proj_kernel(x_ref, wsb_ref, wc_ref, wkr_ref, gq_ref, gkv_ref, wq2_ref, wkv_ref, cos_ref, sin_ref,
                    sbq_ref, sbk_ref, sbv_ref, mq_ref, mk_ref, mv_ref):
    xb = x_ref[...].astype(BF16)
    sb = _dot(xb, wsb_ref[...])
    sbq_ref[...] = sb[:, :D_SB].astype(BF16)
    sbk_ref[...] = sb[:, D_SB:2 * D_SB].astype(BF16)
    sbv_ref[...] = sb[:, 2 * D_SB:].astype(BF16)

    c = _dot(xb, wc_ref[...])
    cq = c[:, :MLA_Q_RANK]
    ckv = c[:, MLA_Q_RANK:]
    nq = cq * lax.rsqrt(jnp.mean(cq * cq, axis=-1, keepdims=True) + RMS_EPS) * gq_ref[...]
    nkv = ckv * lax.rsqrt(jnp.mean(ckv * ckv, axis=-1, keepdims=True) + RMS_EPS) * gkv_ref[...]

    cos = cos_ref[...]
    sin = sin_ref[...]
    kr = _dot(xb, wkr_ref[...])
    k_rope = kr[:, :LANES] * cos + kr[:, LANES:] * sin

    q2 = _dot(nq.astype(BF16), wq2_ref[...])
    kv = _dot(nkv.astype(BF16), wkv_ref[...])
    scale = 1.0 / math.sqrt(MLA_QK_DIM)
    for h in range(MLA_HEADS):
        lo, hi = h * LANES, (h + 1) * LANES
        qa = q2[:, lo:hi]
        qb = q2[:, MLA_HEADS * LANES + lo:MLA_HEADS * LANES + hi]
        mq_ref[:, lo:hi] = ((qa * cos + qb * sin) * scale).astype(BF16)
        mk_ref[:, lo:hi] = (kv[:, lo:hi] + k_rope).astype(BF16)
    mv_ref[...] = kv[:, MLA_HEADS * LANES:].T.astype(BF16)


def _in_proj(x2d, seq_len, wsb, wc, wkr, gq, gkv, wq2, wkv, cos_tab, sin_tab):
    n = x2d.shape[0]
    tm = min(TOKEN_TILE, seq_len)
    pos_tiles = seq_len // tm
    full = lambda a: pl.BlockSpec(a.shape, lambda i: (0,) * a.ndim)
    row = lambda w: pl.BlockSpec((tm, w), lambda i: (i, 0))
    col = lambda r: pl.BlockSpec((r, tm), lambda i: (0, i))
    pos = pl.BlockSpec((tm, LANES), lambda i: (i % pos_tiles, 0))
    wide = MLA_HEADS * LANES
    outs = [(n, D_SB), (n, D_SB), (n, D_SB), (n, wide), (n, wide), (D_MLA_OUT, n)]
    out_specs = [row(D_SB), row(D_SB), row(D_SB), row(wide), row(wide), col(D_MLA_OUT)]
    return pl.pallas_call(
        _in_proj_kernel,
        grid=(n // tm,),
        in_specs=[row(D_MODEL), full(wsb), full(wc), full(wkr), full(gq), full(gkv), full(wq2), full(wkv), pos, pos],
        out_specs=out_specs,
        out_shape=[jax.ShapeDtypeStruct(s, BF16) for s in outs],
        compiler_params=_params(("parallel",)),
        name="in_proj",
    )(x2d, wsb, wc, wkr, gq, gkv, wq2, wkv, cos_tab, sin_tab)


def _sb_attn_kernel(q_ref, k_ref, v_ref, u_ref, o_ref, acc_ref, c_ref, *, tile, pairs):
    qi = pl.program_id(2)
    lane = lax.broadcasted_iota(jnp.int32, (tile, LANES), 1)
    q_heads = []
    for p in range(pairs):
        q2 = q_ref[0, :, p * LANES:(p + 1) * LANES]
        zero = jnp.zeros_like(q2)
        q_heads += [jnp.where(lane < SB_HEAD_DIM, q2, zero), jnp.where(lane >= SB_HEAD_DIM, q2, zero)]
    u = u_ref[...]
    r_id = lax.broadcasted_iota(jnp.int32, (tile, tile), 0)
    c_id = lax.broadcasted_iota(jnp.int32, (tile, tile), 1)
    causal = c_id < r_id

    def sticks(hh, j, mask):
        start = pl.multiple_of(j * tile, tile)
        pair = slice((hh // 2) * LANES, (hh // 2 + 1) * LANES)
        z = _dot_nt(q_heads[hh], k_ref[0, pl.ds(start, tile), pair])
        softplus = jnp.maximum(z, 0.0) + jnp.log1p(jnp.exp(-jnp.abs(z)))
        log_1m = -softplus
        if mask is not None:
            log_1m = jnp.where(mask, log_1m, 0.0)
        hi = log_1m.astype(BF16)
        lo = (log_1m - hi.astype(F32)).astype(BF16)
        return z - softplus, _dot(hi, u) + _dot(lo, u), jnp.sum(log_1m, axis=1, keepdims=True)

    def weighted_values(hh, j, w):
        start = pl.multiple_of(j * tile, tile)
        pair = slice((hh // 2) * LANES, (hh // 2 + 1) * LANES)
        return _dot(w.astype(BF16), v_ref[0, pl.ds(start, tile), pair])

    has_left = qi >= 1
    left = jnp.maximum(qi - 1, 0)
    everywhere = jnp.broadcast_to(has_left, (tile, tile))
    for hh in range(2 * pairs):
        log_beta_d, tail_d, sum_d = sticks(hh, qi, causal)
        log_beta_l, tail_l, sum_l = sticks(hh, left, everywhere)
        w_d = jnp.where(causal, jnp.exp(log_beta_d + tail_d), 0.0)
        w_l = jnp.where(everywhere, jnp.exp(log_beta_l + tail_l + sum_d), 0.0)
        acc_ref[hh] = weighted_values(hh, qi, w_d) + weighted_values(hh, left, w_l)
        c_ref[hh] = sum_d + sum_l

    def block(j):
        for hh in range(2 * pairs):
            log_beta, tail, row_sum = sticks(hh, j, None)
            w = jnp.exp(log_beta + tail + c_ref[hh])
            acc_ref[hh] += weighted_values(hh, j, w)
            c_ref[hh] += row_sum

    def more(carry):
        t, c_max = carry
        return jnp.logical_and(t < qi, c_max > EXP_UNDERFLOW)

    def body(carry):
        t, _ = carry
        block(qi - 1 - t)
        return t + 1, jnp.max(c_ref[...])

    lax.while_loop(more, body, (jnp.int32(1), jnp.max(c_ref[...])))
    for p in range(pairs):
        o_pair = jnp.where(lane < SB_HEAD_DIM, acc_ref[2 * p], acc_ref[2 * p + 1])
        o_ref[0, :, p * LANES:(p + 1) * LANES] = o_pair.astype(o_ref.dtype)


def _sb_attention(q, k, v, u):
    b, s, _ = q.shape
    tile = min(ATTN_TILE, s)
    pairs = SB_PAIRS_PER_STEP
    width = pairs * LANES
    q_spec = pl.BlockSpec((1, tile, width), lambda bi, p, i: (bi, i, p))
    kv_spec = pl.BlockSpec((1, s, width), lambda bi, p, i: (bi, 0, p))
    return pl.pallas_call(
        functools.partial(_sb_attn_kernel, tile=tile, pairs=pairs),
        grid=(b, D_SB // width, s // tile),
        in_specs=[q_spec, kv_spec, kv_spec, pl.BlockSpec((tile, tile), lambda bi, p, i: (0, 0))],
        out_specs=q_spec,
        out_shape=jax.ShapeDtypeStruct((b, s, D_SB), BF16),
        scratch_shapes=[pltpu.VMEM((2 * pairs, tile, LANES), F32), pltpu.VMEM((2 * pairs, tile, 1), F32)],
        compiler_params=_params(("parallel", "parallel", "arbitrary")),
        name="sb_attn",
    )(q, k, v, u)


def _mla_attn_kernel(q_ref, k_ref, vt_ref, o_ref, m_ref, l_ref, acc_ref, *, tile):
    qi = pl.program_id(2)
    q2 = q_ref[0]
    k_id = lax.broadcasted_iota(jnp.int32, (tile, tile), 0)
    q_id = lax.broadcasted_iota(jnp.int32, (tile, tile), 1)
    causal = k_id <= q_id

    m_ref[...] = jnp.full_like(m_ref, -jnp.inf)
    l_ref[...] = jnp.zeros_like(l_ref)
    acc_ref[...] = jnp.zeros_like(acc_ref)

    def block(j, masked):
        start = pl.multiple_of(j * tile, tile)
        kb = k_ref[0, pl.ds(start, tile), :]
        vtb = vt_ref[:, pl.ds(start, tile)]
        for hh in range(2):
            s = _dot_nt(kb[:, hh * LANES:(hh + 1) * LANES], q2[:, hh * LANES:(hh + 1) * LANES])
            if masked:
                s = jnp.where(causal, s, -jnp.inf)
            m_old = m_ref[hh]
            m_new = jnp.maximum(m_old, jnp.max(s, axis=0, keepdims=True))
            p = jnp.exp(s - m_new)
            alpha = jnp.exp(m_old - m_new)
            l_ref[hh] = alpha * l_ref[hh] + jnp.sum(p, axis=0, keepdims=True)
            pv = _dot(vtb[hh * MLA_V_DIM:(hh + 1) * MLA_V_DIM, :], p.astype(BF16))
            acc_ref[hh] = alpha * acc_ref[hh] + pv
            m_ref[hh] = m_new

    block(qi, True)

    def body(j, carry):
        block(j, False)
        return carry

    lax.fori_loop(0, qi, body, 0)
    o_t = jnp.concatenate([acc_ref[0] / l_ref[0], acc_ref[1] / l_ref[1]], axis=0)
    o_ref[0] = o_t.T.astype(o_ref.dtype)


def _mla_attention(q, k, vt):
    b, s, _ = q.shape
    tile = min(MLA_TILE, s)
    pairs = MLA_HEADS // 2
    q_spec = pl.BlockSpec((1, tile, 2 * LANES), lambda bi, p, i: (bi, i, p))
    k_spec = pl.BlockSpec((1, s, 2 * LANES), lambda bi, p, i: (bi, 0, p))
    vt_spec = pl.BlockSpec((LANES, s), lambda bi, p, i: (p, bi))
    o_spec = pl.BlockSpec((1, tile, LANES), lambda bi, p, i: (bi, i, p))
    return pl.pallas_call(
        functools.partial(_mla_attn_kernel, tile=tile),
        grid=(b, pairs, s // tile),
        in_specs=[q_spec, k_spec, vt_spec],
        out_specs=o_spec,
        out_shape=jax.ShapeDtypeStruct((b, s, D_MLA_OUT), BF16),
        scratch_shapes=[pltpu.VMEM((2, 1, tile), F32), pltpu.VMEM((2, 1, tile), F32),
                        pltpu.VMEM((2, MLA_V_DIM, tile), F32)],
        compiler_params=_params(("parallel", "parallel", "arbitrary")),
        name="mla_attn",
    )(q, k, vt)


def _layer_norm(y, g, b):
    mu = jnp.mean(y, axis=-1, keepdims=True)
    d = y - mu
    var = jnp.mean(d * d, axis=-1, keepdims=True)
    return d * lax.rsqrt(var + LN_EPS) * g + b


def _out_proj_kernel(osb_ref, omla_ref, x_ref, wo_ref, g_ref, b_ref, y_ref):
    mix = _dot(osb_ref[...], wo_ref[:D_SB, :]) + _dot(omla_ref[...], wo_ref[D_SB:, :])
    y_ref[...] = _layer_norm(DEEPNORM_ALPHA * x_ref[...] + mix, g_ref[...], b_ref[...])


def _out_proj_ln(osb, omla, x2d, wo, g, b):
    n = x2d.shape[0]
    tm = min(TOKEN_TILE, n)
    full = lambda a: pl.BlockSpec(a.shape, lambda i: (0,) * a.ndim)
    row = lambda w: pl.BlockSpec((tm, w), lambda i: (i, 0))
    return pl.pallas_call(
        _out_proj_kernel,
        grid=(n // tm,),
        in_specs=[row(D_SB), row(D_MLA_OUT), row(D_MODEL), full(wo), full(g), full(b)],
        out_specs=row(D_MODEL),
        out_shape=jax.ShapeDtypeStruct((n, D_MODEL), F32),
        compiler_params=_params(("parallel",)),
        name="out_proj_ln",
    )(osb, omla, x2d, wo, g, b)


def _candidate_tables():
    segs = [(a, PEER_TOPK // (a + 1)) for a in range(8)]
    ids = []
    for a, nb in segs:
        ids += [a * PEER_TOPK + b for b in range(nb)]
    ids += [a * PEER_TOPK for a in range(8, PEER_TOPK)]
    ids += [1000.0, 1001.0]
    ids += [2000.0 + i for i in range(N_CAND - len(ids))]
    return segs, np.asarray(ids, np.float32)


def _extract_top(s, n_take, n_rank, row_id, id_sentinel):
    rank = jnp.full(s.shape, NOT_RANKED, F32)
    vals, ids = [], []
    for r in range(n_take):
        m = jnp.max(s, axis=0, keepdims=True)
        i = jnp.min(jnp.where(s == m, row_id, id_sentinel), axis=0, keepdims=True)
        hit = row_id == i
        s = jnp.where(hit, -jnp.inf, s)
        if r < n_rank:
            rank = jnp.where(hit, float(r), rank)
        vals.append(m)
        ids.append(i)
    return vals, ids, rank


def _extract_values(s, n_take, n_pad=0):
    vals = []
    for _ in range(n_take):
        m = jnp.max(s, axis=0, keepdims=True)
        s = jnp.where(s == m, -jnp.inf, s)
        vals.append(m)
    removed = jnp.sum(jnp.where(s == -jnp.inf, 1.0, 0.0), axis=0, keepdims=True)
    return vals, jnp.where(removed == float(n_take + n_pad), 0.0, 1.0)


def _route_kernel(x_ref, wqt_ref, keys_ref, cid_ref,
                  xt_ref, sa_ref, sb_ref, e1_ref, e2_ref, th_ref, ra_ref, rb_ref, tau_ref, ct_ref, flag_ref,
                  qt_ref, top_ref, cand_ref, *, tile, segs):
    xb = x_ref[...].astype(BF16)
    xt_ref[...] = xb.T
    qt_ref[...] = _dot_nt(wqt_ref[...], xb).astype(BF16)
    key_id = lax.broadcasted_iota(jnp.int32, (PEER_N_KEYS, LANES), 0).astype(F32)
    cand_id = cid_ref[...]

    def head(h, carry):
        rows = pl.ds(pl.multiple_of(h * PEER_N_KEYS, PEER_N_KEYS), PEER_N_KEYS)
        slab = pl.ds(pl.multiple_of(h * SUBLANES, SUBLANES), SUBLANES)

        def select(g, s0, s1, ranks):
            top0, top1, cand = top_ref.at[g, 0], top_ref.at[g, 1], cand_ref.at[g]
            off = 0
            for a, nb in segs:
                cand[off:off + nb, :] = top0[a:a + 1, :] + top1[0:nb, :]
                off += nb
            cand[off:off + 8, :] = top0[8:16, :] + top1[0:1, :]
            off += 8
            cand[off:off + 1, :] = top0[16:17, :] + top1[0:1, :]
            cand[off + 1:off + 2, :] = top0[0:1, :] + top1[16:17, :]
            n_pad = N_CAND - off - 2
            cand[off + 2:, :] = jnp.full((n_pad, LANES), -jnp.inf, F32)
            if ranks is None:
                sums, tied = _extract_values(cand[...], PEER_TOPK + 1, n_pad)
                last_id = jnp.zeros((1, LANES), F32)
            else:
                sums, cids, _ = _extract_top(cand[...], PEER_TOPK + 1, 0, cand_id, 4000.0)
                tied = jnp.where(sums[PEER_TOPK] == sums[PEER_TOPK - 1], 1.0, 0.0)
                last_id = cids[PEER_TOPK - 1]
            z = jnp.zeros_like(sums[0])
            for r in range(PEER_TOPK):
                z = z + jnp.exp(sums[r] - sums[0])
            tau = sums[PEER_TOPK - 1]
            e1_ref[g, rows, :] = jnp.exp(s0 - top0[0:1, :]) * (0.5 / z)
            e2_ref[g, rows, :] = jnp.exp(s1 - top1[0:1, :])
            vb = top1[0:PEER_TOPK, :]
            theta = jnp.full((PEER_N_KEYS, LANES), jnp.inf, F32)
            for a in range(PEER_TOPK):
                va = top0[a:a + 1, :]
                th_a = jnp.min(jnp.where(va + vb >= tau, vb, jnp.inf), axis=0, keepdims=True)
                match = (s0 == va) if ranks is None else (ranks[0] == float(a))
                theta = jnp.where(match, th_a, theta)
            th_ref[g, rows, :] = theta
            tau_ref[g, slab, :] = jnp.broadcast_to(tau, (SUBLANES, LANES))
            ct_ref[g, slab, :] = jnp.broadcast_to(last_id, (SUBLANES, LANES))
            flag_ref[g, slab, :] = jnp.broadcast_to(tied, (SUBLANES, LANES))
            return tied

        redo = []
        for g in range(tile // LANES):
            lanes = slice(g * LANES, (g + 1) * LANES)
            scores = []
            repeated = jnp.zeros((1, LANES), F32)
            for c in range(2):
                hc = h * 2 + c
                q = qt_ref[pl.ds(pl.multiple_of(hc * PEER_HALF, PEER_HALF), PEER_HALF), lanes]
                s = _dot(keys_ref[hc], q)
                vals, rep = _extract_values(s, PEER_TOPK + 1)
                for r, v in enumerate(vals):
                    top_ref[g, c, r:r + 1, :] = v
                repeated = jnp.maximum(repeated, rep)
                scores.append(s)
            s0, s1 = scores
            sa_ref[g, rows, :] = s0
            sb_ref[g, rows, :] = s1
            ra_ref[g, rows, :] = jnp.full((PEER_N_KEYS, LANES), NOT_RANKED, F32)
            rb_ref[g, rows, :] = jnp.full((PEER_N_KEYS, LANES), NOT_RANKED, rb_ref.dtype)
            tied = select(g, s0, s1, None)
            redo.append(jnp.max(jnp.maximum(repeated, tied)) > 0.0)

        def exact_group(g):
            ranks = []
            for c, sc in enumerate((sa_ref[g, rows, :], sb_ref[g, rows, :])):
                vals, _, rank = _extract_top(sc, PEER_TOPK + 1, PEER_TOPK, key_id, float(PEER_N_KEYS))
                for r, v in enumerate(vals):
                    top_ref[g, c, r:r + 1, :] = v
                ranks.append(rank)
            ra_ref[g, rows, :] = ranks[0] * float(PEER_TOPK)
            rb_ref[g, rows, :] = ranks[1].astype(rb_ref.dtype)
            select(g, sa_ref[g, rows, :], sb_ref[g, rows, :], ranks)

        for g in range(tile // LANES):
            pl.when(redo[g])(functools.partial(exact_group, g))
        return carry

    lax.fori_loop(0, PEER_HEADS, head, 0)


def _peer_route(x2d, wqt, keys, cand_ids):
    n = x2d.shape[0]
    tile = min(ROUTE_TILE, n)
    segs, _ = _candidate_tables()
    hk = PEER_HEADS * PEER_N_KEYS
    full = lambda a: pl.BlockSpec(a.shape, lambda i: (0,) * a.ndim)
    groups = tile // LANES
    slab = lambda r: pl.BlockSpec((groups, r, LANES), lambda i: (i, 0, 0))
    slab_shape = lambda r: (n // LANES, r, LANES)
    hs = PEER_HEADS * SUBLANES
    outs = ([((D_MODEL, n), BF16)] + [(slab_shape(hk), F32)] * 6 + [(slab_shape(hk), BF16)]
            + [(slab_shape(hs), F32)] * 3)
    out_specs = ([pl.BlockSpec((D_MODEL, tile), lambda i: (0, i))] + [slab(hk)] * 7 + [slab(hs)] * 3)
    return pl.pallas_call(
        functools.partial(_route_kernel, tile=tile, segs=segs),
        grid=(n // tile,),
        in_specs=[pl.BlockSpec((tile, D_MODEL), lambda i: (i, 0)), full(wqt), full(keys), full(cand_ids)],
        out_specs=out_specs,
        out_shape=[jax.ShapeDtypeStruct(s, d) for s, d in outs],
        scratch_shapes=[pltpu.VMEM((2 * PEER_HEADS * PEER_HALF, tile), BF16),
                        pltpu.VMEM((groups, 2, 24, LANES), F32),
                        pltpu.VMEM((groups, N_CAND, LANES), F32)],
        compiler_params=_params(("parallel",)),
        name="peer_route",
    )(x2d, wqt, keys, cand_ids)


def _twice_gelu(h):
    return h * (1.0 + lax.erf(h * (1.0 / math.sqrt(2.0))))


def _dense_kernel(xt_ref, u_ref, vt_ref, sa_ref, sb_ref, e1_ref, e2_ref, th_ref, ra_ref, rb_ref, tau_ref, ct_ref,
                  flag_ref, x_ref, g_ref, b_ref, y_ref, acc_ref, h_ref, w_ref, tied_ref, *, tile, etile):
    s = pl.program_id(1)
    n_tiles = pl.num_programs(1) - 1

    @pl.when(s == 0)
    def _():
        acc_ref[...] = jnp.zeros_like(acc_ref)
        tied_ref[0] = (jnp.max(flag_ref[...]) > 0.0).astype(jnp.int32)
        h_ref[0] = _dot(u_ref[...], xt_ref[...])

    tied = tied_ref[0] > 0

    def step(parity, exact):
        cur, nxt = 1 - parity, parity
        h_ref[nxt] = _dot(u_ref[...], xt_ref[...])
        down = None
        for r in range(etile // PEER_N_KEYS):
            i0 = (s - 1) * (etile // PEER_N_KEYS) + r
            for g in range(tile // LANES):
                lanes = slice(g * LANES, (g + 1) * LANES)
                row_of = lambda ref, h: jnp.broadcast_to(ref[g, pl.ds(h * PEER_N_KEYS + i0, 1), :], (SUBLANES, LANES))
                head_of = lambda ref, h: ref[g, h * SUBLANES:(h + 1) * SUBLANES, :]
                e1_b = [row_of(e1_ref, h) for h in range(PEER_HEADS)]
                if exact:
                    a_b = [row_of(sa_ref, h) for h in range(PEER_HEADS)]
                    ra_b = [row_of(ra_ref, h) for h in range(PEER_HEADS)]
                    tau_b = [head_of(tau_ref, h) for h in range(PEER_HEADS)]
                    ct_b = [head_of(ct_ref, h) for h in range(PEER_HEADS)]
                else:
                    th_b = [row_of(th_ref, h) for h in range(PEER_HEADS)]
                for jc in range(PEER_N_KEYS // GATE_ROWS):
                    if exact:
                        rb = [rb_ref[g, h * PEER_N_KEYS + jc * GATE_ROWS:h * PEER_N_KEYS + (jc + 1) * GATE_ROWS, :]
                              .astype(F32) for h in range(PEER_HEADS)]
                    parts = []
                    for sub in range(GATE_ROWS // SUBLANES):
                        j0 = jc * GATE_ROWS + sub * SUBLANES
                        gate = None
                        for h in range(PEER_HEADS):
                            rows = slice(h * PEER_N_KEYS + j0, h * PEER_N_KEYS + j0 + SUBLANES)
                            if exact:
                                total = a_b[h] + sb_ref[g, rows, :]
                                cand = ra_b[h] + rb[h][sub * SUBLANES:(sub + 1) * SUBLANES]
                                keep = (total > tau_b[h]) | ((total == tau_b[h]) & (cand <= ct_b[h]))
                            else:
                                keep = sb_ref[g, rows, :] >= th_b[h]
                            term = jnp.where(keep, e1_b[h] * e2_ref[g, rows, :], 0.0)
                            gate = term if gate is None else gate + term
                        hv = h_ref[cur, r * PEER_N_KEYS + j0:r * PEER_N_KEYS + j0 + SUBLANES, lanes]
                        parts.append(gate * _twice_gelu(hv))
                    rows = slice(r * PEER_N_KEYS + jc * GATE_ROWS, r * PEER_N_KEYS + (jc + 1) * GATE_ROWS)
                    w_ref[rows, lanes] = jnp.concatenate(parts, axis=0).astype(BF16)
            block = slice(r * PEER_N_KEYS, (r + 1) * PEER_N_KEYS)
            part = _dot(vt_ref[:, block], w_ref[block, :])
            down = part if down is None else down + part
        acc_ref[...] += down

    for parity in range(2):
        for exact in (False, True):
            cond = jnp.logical_and(jnp.logical_and(s >= 1, lax.rem(s, 2) == parity), tied == exact)
            pl.when(cond)(functools.partial(step, parity, exact))

    @pl.when(s == n_tiles)
    def _():
        y = DEEPNORM_ALPHA * x_ref[...] + acc_ref[...].T
        y_ref[...] = _layer_norm(y, g_ref[...], b_ref[...])


def _peer_dense(xt, u, vt, route, x2d, g, b):
    n = x2d.shape[0]
    tile = min(DENSE_TOKEN_TILE, n)
    etile = DENSE_EXPERT_TILE
    steps = PEER_N_EXPERTS // etile
    hk = PEER_HEADS * PEER_N_KEYS
    slab = lambda r: pl.BlockSpec((tile // LANES, r, LANES), lambda i, e: (i, 0, 0))
    full = lambda a: pl.BlockSpec(a.shape, lambda i, e: (0,) * a.ndim)
    return pl.pallas_call(
        functools.partial(_dense_kernel, tile=tile, etile=etile),
        grid=(n // tile, steps + 1),
        in_specs=[pl.BlockSpec((D_MODEL, tile), lambda i, e: (0, i)),
                  pl.BlockSpec((etile, D_MODEL), lambda i, e: (jnp.minimum(e, steps - 1), 0)),
                  pl.BlockSpec((D_MODEL, etile), lambda i, e: (0, jnp.maximum(e - 1, 0)))]
                 + [slab(hk)] * 7 + [slab(PEER_HEADS * SUBLANES)] * 3
                 + [pl.BlockSpec((tile, D_MODEL), lambda i, e: (i, 0)), full(g), full(b)],
        out_specs=pl.BlockSpec((tile, D_MODEL), lambda i, e: (i, 0)),
        out_shape=jax.ShapeDtypeStruct((n, D_MODEL), F32),
        scratch_shapes=[pltpu.VMEM((D_MODEL, tile), F32), pltpu.VMEM((2, etile, tile), F32),
                        pltpu.VMEM((etile, tile), BF16), pltpu.SMEM((1,), jnp.int32)],
        compiler_params=_params(("parallel", "arbitrary")),
        name="peer_dense",
    )(xt, u, vt, *route, x2d, g, b)


def _rope_tables(seq_len):
    pos = jnp.arange(seq_len, dtype=F32)
    inv_freq = ROPE_THETA ** (-jnp.arange(0, MLA_ROPE_DIM, 2, dtype=F32) / MLA_ROPE_DIM)
    ang = pos[:, None] * inv_freq[None, :]
    cos, sin = jnp.cos(ang), jnp.sin(ang)
    ones = jnp.ones((seq_len, MLA_NOPE_DIM), F32)
    pad = jnp.zeros((seq_len, LANES - MLA_QK_DIM), F32)
    cos_tab = jnp.concatenate([ones, cos, cos, pad], axis=1)
    sin_tab = jnp.concatenate([0.0 * ones, -sin, sin, pad], axis=1)
    return cos_tab, sin_tab


def _rotate_half_columns(w):
    half = MLA_ROPE_DIM // 2
    return jnp.concatenate([w[..., half:], w[..., :half]], axis=-1)


def _mixer_weights(w_in, w_uq, w_ukv):
    sb_scale = 1.0 / math.sqrt(SB_HEAD_DIM)
    wsb = jnp.concatenate([w_in[:, :D_SB] * sb_scale, w_in[:, D_SB:3 * D_SB]], axis=1).astype(BF16)
    wc = w_in[:, 3 * D_SB:3 * D_SB + MLA_Q_RANK + MLA_KV_RANK].astype(BF16)
    w_kr = w_in[:, 3 * D_SB + MLA_Q_RANK + MLA_KV_RANK:]
    lead = jnp.zeros((D_MODEL, MLA_NOPE_DIM), F32)
    trail = jnp.zeros((D_MODEL, LANES - MLA_QK_DIM), F32)
    wkr = jnp.concatenate([lead, w_kr, trail, lead, _rotate_half_columns(w_kr), trail], axis=1).astype(BF16)

    wq = w_uq.reshape(MLA_Q_RANK, MLA_HEADS, MLA_QK_DIM)
    qpad = jnp.zeros((MLA_Q_RANK, MLA_HEADS, LANES - MLA_QK_DIM), F32)
    q_plain = jnp.concatenate([wq, qpad], axis=2)
    q_rot = jnp.concatenate([0.0 * wq[..., :MLA_NOPE_DIM], _rotate_half_columns(wq[..., MLA_NOPE_DIM:]), qpad], axis=2)
    wq2 = jnp.concatenate([q_plain.reshape(MLA_Q_RANK, -1), q_rot.reshape(MLA_Q_RANK, -1)], axis=1).astype(BF16)

    wkv = w_ukv.reshape(MLA_KV_RANK, MLA_HEADS, MLA_NOPE_DIM + MLA_V_DIM)
    kpad = jnp.zeros((MLA_KV_RANK, MLA_HEADS, LANES - MLA_NOPE_DIM), F32)
    wk = jnp.concatenate([wkv[..., :MLA_NOPE_DIM], kpad], axis=2).reshape(MLA_KV_RANK, -1)
    wv = wkv[..., MLA_NOPE_DIM:].reshape(MLA_KV_RANK, -1)
    wkv2 = jnp.concatenate([wk, wv], axis=1).astype(BF16)
    return wsb, wc, wkr, wq2, wkv2


def kernel(x, w_in, g_cq, w_uq, g_ckv, w_ukv, w_o, ln1_g, ln1_b, peer_wq, peer_keys, peer_u, peer_v, ln2_g, ln2_b):
    b, s, d = x.shape
    n = b * s
    cos_tab, sin_tab = _rope_tables(s)
    tile = min(ATTN_TILE, s)
    strict_upper = (jnp.arange(tile)[:, None] > jnp.arange(tile)[None, :]).astype(BF16)
    _, cand_ids = _candidate_tables()
    cand_ids = jnp.asarray(np.broadcast_to(cand_ids[:, None], (N_CAND, LANES)).copy())

    x2d = x.reshape(n, d)
    for l in range(DEPTH):
        wsb, wc, wkr, wq2, wkv2 = _mixer_weights(w_in[l], w_uq[l], w_ukv[l])
        sbq, sbk, sbv, mq, mk, mvt = _in_proj(x2d, s, wsb, wc, wkr, g_cq[l][None, :], g_ckv[l][None, :], wq2, wkv2,
                                              cos_tab, sin_tab)
        sh = lambda a: a.reshape(b, s, a.shape[-1])
        o_sb = _sb_attention(sh(sbq), sh(sbk), sh(sbv), strict_upper)
        o_mla = _mla_attention(sh(mq), sh(mk), mvt)
        x2d = _out_proj_ln(o_sb.reshape(n, D_SB), o_mla.reshape(n, D_MLA_OUT), x2d, w_o[l].astype(BF16),
                           ln1_g[l][None, :], ln1_b[l][None, :])
        wqt = peer_wq[l].T.astype(BF16)
        keys = peer_keys[l].reshape(PEER_HEADS * 2, PEER_N_KEYS, PEER_HALF).astype(BF16)
        route = _peer_route(x2d, wqt, keys, cand_ids)
        x2d = _peer_dense(route[0], peer_u[l].astype(BF16), peer_v[l].T.astype(BF16), route[1:], x2d,
                          ln2_g[l][None, :], ln2_b[l][None, :])
    return x2d.reshape(b, s, d)
```

```python
import functools
import math

import jax
import jax.numpy as jnp
import numpy as np
from jax import lax
from jax.experimental import pallas as pl
from jax.experimental.pallas import tpu as pltpu

D_MODEL = 1024
DEPTH = 2
SB_HEADS = 8
SB_HEAD_DIM = 64
D_SB = SB_HEADS * SB_HEAD_DIM
MLA_HEADS = 8
MLA_NOPE_DIM = 64
MLA_ROPE_DIM = 32
MLA_QK_DIM = MLA_NOPE_DIM + MLA_ROPE_DIM
MLA_V_DIM = 64
MLA_Q_RANK = 256
MLA_KV_RANK = 128
D_MLA_OUT = MLA_HEADS * MLA_V_DIM
ROPE_THETA = 10000.0
PEER_HEADS = 8
PEER_N_KEYS = 128
PEER_N_EXPERTS = PEER_N_KEYS * PEER_N_KEYS
PEER_HALF = 128
PEER_TOPK = 16
DEEPNORM_ALPHA = (2.0 * DEPTH) ** 0.25
LN_EPS = 1e-5
RMS_EPS = 1e-6

LANES = 128
SUBLANES = 8
VMEM_LIMIT_BYTES = 56 * 1024 * 1024

TOKEN_TILE = 512
ATTN_TILE = 256
SB_PAIRS_PER_STEP = 4
MLA_TILE = 1024
ROUTE_TILE = 512
DENSE_TOKEN_TILE = 256
DENSE_EXPERT_TILE = 2048
GATE_ROWS = 16

EXP_UNDERFLOW = -120.0
NOT_RANKED = 4096.0
N_CAND = 56

F32 = jnp.float32
BF16 = jnp.bfloat16


def _dot(a, b):
    return jnp.dot(a, b, preferred_element_type=F32)


def _dot_nt(a, b):
    return lax.dot_general(a, b, (((1,), (1,)), ((), ())), preferred_element_type=F32)


def _params(semantics, flags=None):
    return pltpu.CompilerParams(dimension_semantics=semantics, vmem_limit_bytes=VMEM_LIMIT_BYTES, flags=flags)


def _in_proj_kernel(x_ref, wsb_ref, wc_ref, wkr_ref, gq_ref, gkv_ref, wq2_ref, wkv_ref, cos_ref, sin_ref,
                    sbq_ref, sbk_ref, sbv_ref, mq_ref, mk_ref, mv_ref):
    xb = x_ref[...].astype(BF16)
    sb = _dot(xb, wsb_ref[...])
    sbq_ref[...] = sb[:, :D_SB].astype(BF16)
    sbk_ref[...] = sb[:, D_SB:2 * D_SB].astype(BF16)
    sbv_ref[...] = sb[:, 2 * D_SB:].astype(BF16)

    c = _dot(xb, wc_ref[...])
    cq = c[:, :MLA_Q_RANK]
    ckv = c[:, MLA_Q_RANK:]
    nq = cq * lax.rsqrt(jnp.mean(cq * cq, axis=-1, keepdims=True) + RMS_EPS) * gq_ref[...]
    nkv = ckv * lax.rsqrt(jnp.mean(ckv * ckv, axis=-1, keepdims=True) + RMS_EPS) * gkv_ref[...]

    cos = cos_ref[...]
    sin = sin_ref[...]
    kr = _dot(xb, wkr_ref[...])
    k_rope = kr[:, :LANES] * cos + kr[:, LANES:] * sin

    q2 = _dot(nq.astype(BF16), wq2_ref[...])
    kv = _dot(nkv.astype(BF16), wkv_ref[...])
    scale = 1.0 / math.sqrt(MLA_QK_DIM)
    for h in range(MLA_HEADS):
        lo, hi = h * LANES, (h + 1) * LANES
        qa = q2[:, lo:hi]
        qb = q2[:, MLA_HEADS * LANES + lo:MLA_HEADS * LANES + hi]
        mq_ref[:, lo:hi] = ((qa * cos + qb * sin) * scale).astype(BF16)
        mk_ref[:, lo:hi] = (kv[:, lo:hi] + k_rope).astype(BF16)
    mv_ref[...] = kv[:, MLA_HEADS * LANES:].T.astype(BF16)


def _in_proj(x2d, seq_len, wsb, wc, wkr, gq, gkv, wq2, wkv, cos_tab, sin_tab):
    n = x2d.shape[0]
    tm = min(TOKEN_TILE, seq_len)
    pos_tiles = seq_len // tm
    full = lambda a: pl.BlockSpec(a.shape, lambda i: (0,) * a.ndim)
    row = lambda w: pl.BlockSpec((tm, w), lambda i: (i, 0))
    col = lambda r: pl.BlockSpec((r, tm), lambda i: (0, i))
    pos = pl.BlockSpec((tm, LANES), lambda i: (i % pos_tiles, 0))
    wide = MLA_HEADS * LANES
    outs = [(n, D_SB), (n, D_SB), (n, D_SB), (n, wide), (n, wide), (D_MLA_OUT, n)]
    out_specs = [row(D_SB), row(D_SB), row(D_SB), row(wide), row(wide), col(D_MLA_OUT)]
    return pl.pallas_call(
        _in_proj_kernel,
        grid=(n // tm,),
        in_specs=[row(D_MODEL), full(wsb), full(wc), full(wkr), full(gq), full(gkv), full(wq2), full(wkv), pos, pos],
        out_specs=out_specs,
        out_shape=[jax.ShapeDtypeStruct(s, BF16) for s in outs],
        compiler_params=_params(("parallel",)),
        name="in_proj",
    )(x2d, wsb, wc, wkr, gq, gkv, wq2, wkv, cos_tab, sin_tab)


def _sb_attn_kernel(q_ref, k_ref, v_ref, u_ref, o_ref, acc_ref, c_ref, *, tile, pairs):
    qi = pl.program_id(2)
    lane = lax.broadcasted_iota(jnp.int32, (tile, LANES), 1)
    q_heads = []
    for p in range(pairs):
        q2 = q_ref[0, :, p * LANES:(p + 1) * LANES]
        zero = jnp.zeros_like(q2)
        q_heads += [jnp.where(lane < SB_HEAD_DIM, q2, zero), jnp.where(lane >= SB_HEAD_DIM, q2, zero)]
    u = u_ref[...]
    r_id = lax.broadcasted_iota(jnp.int32, (tile, tile), 0)
    c_id = lax.broadcasted_iota(jnp.int32, (tile, tile), 1)
    causal = c_id < r_id

    def sticks(hh, j, mask):
        start = pl.multiple_of(j * tile, tile)
        pair = slice((hh // 2) * LANES, (hh // 2 + 1) * LANES)
        z = _dot_nt(q_heads[hh], k_ref[0, pl.ds(start, tile), pair])
        softplus = jnp.maximum(z, 0.0) + jnp.log1p(jnp.exp(-jnp.abs(z)))
        log_1m = -softplus
        if mask is not None:
            log_1m = jnp.where(mask, log_1m, 0.0)
        hi = log_1m.astype(BF16)
        lo = (log_1m - hi.astype(F32)).astype(BF16)
        return z - softplus, _dot(hi, u) + _dot(lo, u), jnp.sum(log_1m, axis=1, keepdims=True)

    def weighted_values(hh, j, w):
        start = pl.multiple_of(j * tile, tile)
        pair = slice((hh // 2) * LANES, (hh // 2 + 1) * LANES)
        return _dot(w.astype(BF16), v_ref[0, pl.ds(start, tile), pair])

    has_left = qi >= 1
    left = jnp.maximum(qi - 1, 0)
    everywhere = jnp.broadcast_to(has_left, (tile, tile))
    for hh in range(2 * pairs):
        log_beta_d, tail_d, sum_d = sticks(hh, qi, causal)
        log_beta_l, tail_l, sum_l = sticks(hh, left, everywhere)
        w_d = jnp.where(causal, jnp.exp(log_beta_d + tail_d), 0.0)
        w_l = jnp.where(everywhere, jnp.exp(log_beta_l + tail_l + sum_d), 0.0)
        acc_ref[hh] = weighted_values(hh, qi, w_d) + weighted_values(hh, left, w_l)
        c_ref[hh] = sum_d + sum_l

    def block(j):
        for hh in range(2 * pairs):
            log_beta, tail, row_sum = sticks(hh, j, None)
            w = jnp.exp(log_beta + tail + c_ref[hh])
            acc_ref[hh] += weighted_values(hh, j, w)
            c_ref[hh] += row_sum

    def more(carry):
        t, c_max = carry
        return jnp.logical_and(t < qi, c_max > EXP_UNDERFLOW)

    def body(carry):
        t, _ = carry
        block(qi - 1 - t)
        return t + 1, jnp.max(c_ref[...])

    lax.while_loop(more, body, (jnp.int32(1), jnp.max(c_ref[...])))
    for p in range(pairs):
        o_pair = jnp.where(lane < SB_HEAD_DIM, acc_ref[2 * p], acc_ref[2 * p + 1])
        o_ref[0, :, p * LANES:(p + 1) * LANES] = o_pair.astype(o_ref.dtype)


def _sb_attention(q, k, v, u):
    b, s, _ = q.shape
    tile = min(ATTN_TILE, s)
    pairs = SB_PAIRS_PER_STEP
    width = pairs * LANES
    q_spec = pl.BlockSpec((1, tile, width), lambda bi, p, i: (bi, i, p))
    kv_spec = pl.BlockSpec((1, s, width), lambda bi, p, i: (bi, 0, p))
    return pl.pallas_call(
        functools.partial(_sb_attn_kernel, tile=tile, pairs=pairs),
        grid=(b, D_SB // width, s // tile),
        in_specs=[q_spec, kv_spec, kv_spec, pl.BlockSpec((tile, tile), lambda bi, p, i: (0, 0))],
        out_specs=q_spec,
        out_shape=jax.ShapeDtypeStruct((b, s, D_SB), BF16),
        scratch_shapes=[pltpu.VMEM((2 * pairs, tile, LANES), F32), pltpu.VMEM((2 * pairs, tile, 1), F32)],
        compiler_params=_params(("parallel", "parallel", "arbitrary")),
        name="sb_attn",
    )(q, k, v, u)


def _mla_attn_kernel(q_ref, k_ref, vt_ref, o_ref, m_ref, l_ref, acc_ref, *, tile):
    qi = pl.program_id(2)
    q2 = q_ref[0]
    k_id = lax.broadcasted_iota(jnp.int32, (tile, tile), 0)
    q_id = lax.broadcasted_iota(jnp.int32, (tile, tile), 1)
    causal = k_id <= q_id

    m_ref[...] = jnp.full_like(m_ref, -jnp.inf)
    l_ref[...] = jnp.zeros_like(l_ref)
    acc_ref[...] = jnp.zeros_like(acc_ref)

    def block(j, masked):
        start = pl.multiple_of(j * tile, tile)
        kb = k_ref[0, pl.ds(start, tile), :]
        vtb = vt_ref[:, pl.ds(start, tile)]
        for hh in range(2):
            s = _dot_nt(kb[:, hh * LANES:(hh + 1) * LANES], q2[:, hh * LANES:(hh + 1) * LANES])
            if masked:
                s = jnp.where(causal, s, -jnp.inf)
            m_old = m_ref[hh]
            m_new = jnp.maximum(m_old, jnp.max(s, axis=0, keepdims=True))
            p = jnp.exp(s - m_new)
            alpha = jnp.exp(m_old - m_new)
            l_ref[hh] = alpha * l_ref[hh] + jnp.sum(p, axis=0, keepdims=True)
            pv = _dot(vtb[hh * MLA_V_DIM:(hh + 1) * MLA_V_DIM, :], p.astype(BF16))
            acc_ref[hh] = alpha * acc_ref[hh] + pv
            m_ref[hh] = m_new

    block(qi, True)

    def body(j, carry):
        block(j, False)
        return carry

    lax.fori_loop(0, qi, body, 0)
    o_t = jnp.concatenate([acc_ref[0] / l_ref[0], acc_ref[1] / l_ref[1]], axis=0)
    o_ref[0] = o_t.T.astype(o_ref.dtype)


def _mla_attention(q, k, vt):
    b, s, _ = q.shape
    tile = min(MLA_TILE, s)
    pairs = MLA_HEADS // 2
    q_spec = pl.BlockSpec((1, tile, 2 * LANES), lambda bi, p, i: (bi, i, p))
    k_spec = pl.BlockSpec((1, s, 2 * LANES), lambda bi, p, i: (bi, 0, p))
    vt_spec = pl.BlockSpec((LANES, s), lambda bi, p, i: (p, bi))
    o_spec = pl.BlockSpec((1, tile, LANES), lambda bi, p, i: (bi, i, p))
    return pl.pallas_call(
        functools.partial(_mla_attn_kernel, tile=tile),
        grid=(b, pairs, s // tile),
        in_specs=[q_spec, k_spec, vt_spec],
        out_specs=o_spec,
        out_shape=jax.ShapeDtypeStruct((b, s, D_MLA_OUT), BF16),
        scratch_shapes=[pltpu.VMEM((2, 1, tile), F32), pltpu.VMEM((2, 1, tile), F32),
                        pltpu.VMEM((2, MLA_V_DIM, tile), F32)],
        compiler_params=_params(("parallel", "parallel", "arbitrary")),
        name="mla_attn",
    )(q, k, vt)


def _layer_norm(y, g, b):
    mu = jnp.mean(y, axis=-1, keepdims=True)
    d = y - mu
    var = jnp.mean(d * d, axis=-1, keepdims=True)
    return d * lax.rsqrt(var + LN_EPS) * g + b


def _out_proj_kernel(osb_ref, omla_ref, x_ref, wo_ref, g_ref, b_ref, y_ref):
    mix = _dot(osb_ref[...], wo_ref[:D_SB, :]) + _dot(omla_ref[...], wo_ref[D_SB:, :])
    y_ref[...] = _layer_norm(DEEPNORM_ALPHA * x_ref[...] + mix, g_ref[...], b_ref[...])


def _out_proj_ln(osb, omla, x2d, wo, g, b):
    n = x2d.shape[0]
    tm = min(TOKEN_TILE, n)
    full = lambda a: pl.BlockSpec(a.shape, lambda i: (0,) * a.ndim)
    row = lambda w: pl.BlockSpec((tm, w), lambda i: (i, 0))
    return pl.pallas_call(
        _out_proj_kernel,
        grid=(n // tm,),
        in_specs=[row(D_SB), row(D_MLA_OUT), row(D_MODEL), full(wo), full(g), full(b)],
        out_specs=row(D_MODEL),
        out_shape=jax.ShapeDtypeStruct((n, D_MODEL), F32),
        compiler_params=_params(("parallel",)),
        name="out_proj_ln",
    )(osb, omla, x2d, wo, g, b)


def _candidate_tables():
    segs = [(a, PEER_TOPK // (a + 1)) for a in range(8)]
    ids = []
    for a, nb in segs:
        ids += [a * PEER_TOPK + b for b in range(nb)]
    ids += [a * PEER_TOPK for a in range(8, PEER_TOPK)]
    ids += [1000.0, 1001.0]
    ids += [2000.0 + i for i in range(N_CAND - len(ids))]
    return segs, np.asarray(ids, np.float32)


def _extract_top(s, n_take, n_rank, row_id, id_sentinel):
    rank = jnp.full(s.shape, NOT_RANKED, F32)
    vals, ids = [], []
    for r in range(n_take):
        m = jnp.max(s, axis=0, keepdims=True)
        i = jnp.min(jnp.where(s == m, row_id, id_sentinel), axis=0, keepdims=True)
        hit = row_id == i
        s = jnp.where(hit, -jnp.inf, s)
        if r < n_rank:
            rank = jnp.where(hit, float(r), rank)
        vals.append(m)
        ids.append(i)
    return vals, ids, rank


def _extract_values(s, n_take, n_pad=0):
    vals = []
    for _ in range(n_take):
        m = jnp.max(s, axis=0, keepdims=True)
        s = jnp.where(s == m, -jnp.inf, s)
        vals.append(m)
    removed = jnp.sum(jnp.where(s == -jnp.inf, 1.0, 0.0), axis=0, keepdims=True)
    return vals, jnp.where(removed == float(n_take + n_pad), 0.0, 1.0)


def _route_kernel(x_ref, wqt_ref, keys_ref, cid_ref,
                  xt_ref, sa_ref, sb_ref, e1_ref, e2_ref, th_ref, ra_ref, rb_ref, tau_ref, ct_ref, flag_ref,
                  qt_ref, top_ref, cand_ref, *, tile, segs):
    xb = x_ref[...].astype(BF16)
    xt_ref[...] = xb.T
    qt_ref[...] = _dot_nt(wqt_ref[...], xb).astype(BF16)
    key_id = lax.broadcasted_iota(jnp.int32, (PEER_N_KEYS, LANES), 0).astype(F32)
    cand_id = cid_ref[...]

    def head(h, carry):
        rows = pl.ds(pl.multiple_of(h * PEER_N_KEYS, PEER_N_KEYS), PEER_N_KEYS)
        slab = pl.ds(pl.multiple_of(h * SUBLANES, SUBLANES), SUBLANES)

        def select(g, s0, s1, ranks):
            top0, top1, cand = top_ref.at[g, 0], top_ref.at[g, 1], cand_ref.at[g]
            off = 0
            for a, nb in segs:
                cand[off:off + nb, :] = top0[a:a + 1, :] + top1[0:nb, :]
                off += nb
            cand[off:off + 8, :] = top0[8:16, :] + top1[0:1, :]
            off += 8
            cand[off:off + 1, :] = top0[16:17, :] + top1[0:1, :]
            cand[off + 1:off + 2, :] = top0[0:1, :] + top1[16:17, :]
            n_pad = N_CAND - off - 2
            cand[off + 2:, :] = jnp.full((n_pad, LANES), -jnp.inf, F32)
            if ranks is None:
                sums, tied = _extract_values(cand[...], PEER_TOPK + 1, n_pad)
                last_id = jnp.zeros((1, LANES), F32)
            else:
                sums, cids, _ = _extract_top(cand[...], PEER_TOPK + 1, 0, cand_id, 4000.0)
                tied = jnp.where(sums[PEER_TOPK] == sums[PEER_TOPK - 1], 1.0, 0.0)
                last_id = cids[PEER_TOPK - 1]
            z = jnp.zeros_like(sums[0])
            for r in range(PEER_TOPK):
                z = z + jnp.exp(sums[r] - sums[0])
            tau = sums[PEER_TOPK - 1]
            e1_ref[g, rows, :] = jnp.exp(s0 - top0[0:1, :]) * (0.5 / z)
            e2_ref[g, rows, :] = jnp.exp(s1 - top1[0:1, :])
            vb = top1[0:PEER_TOPK, :]
            theta = jnp.full((PEER_N_KEYS, LANES), jnp.inf, F32)
            for a in range(PEER_TOPK):
                va = top0[a:a + 1, :]
                th_a = jnp.min(jnp.where(va + vb >= tau, vb, jnp.inf), axis=0, keepdims=True)
                match = (s0 == va) if ranks is None else (ranks[0] == float(a))
                theta = jnp.where(match, th_a, theta)
            th_ref[g, rows, :] = theta
            tau_ref[g, slab, :] = jnp.broadcast_to(tau, (SUBLANES, LANES))
            ct_ref[g, slab, :] = jnp.broadcast_to(last_id, (SUBLANES, LANES))
            flag_ref[g, slab, :] = jnp.broadcast_to(tied, (SUBLANES, LANES))
            return tied

        redo = []
        for g in range(tile // LANES):
            lanes = slice(g * LANES, (g + 1) * LANES)
            scores = []
            repeated = jnp.zeros((1, LANES), F32)
            for c in range(2):
                hc = h * 2 + c
                q = qt_ref[pl.ds(pl.multiple_of(hc * PEER_HALF, PEER_HALF), PEER_HALF), lanes]
                s = _dot(keys_ref[hc], q)
                vals, rep = _extract_values(s, PEER_TOPK + 1)
                for r, v in enumerate(vals):
                    top_ref[g, c, r:r + 1, :] = v
                repeated = jnp.maximum(repeated, rep)
                scores.append(s)
            s0, s1 = scores
            sa_ref[g, rows, :] = s0
            sb_ref[g, rows, :] = s1
            ra_ref[g, rows, :] = jnp.full((PEER_N_KEYS, LANES), NOT_RANKED, F32)
            rb_ref[g, rows, :] = jnp.full((PEER_N_KEYS, LANES), NOT_RANKED, rb_ref.dtype)
            tied = select(g, s0, s1, None)
            redo.append(jnp.max(jnp.maximum(repeated, tied)) > 0.0)

        def exact_group(g):
            ranks = []
            for c, sc in enumerate((sa_ref[g, rows, :], sb_ref[g, rows, :])):
                vals, _, rank = _extract_top(sc, PEER_TOPK + 1, PEER_TOPK, key_id, float(PEER_N_KEYS))
                for r, v in enumerate(vals):
                    top_ref[g, c, r:r + 1, :] = v
                ranks.append(rank)
            ra_ref[g, rows, :] = ranks[0] * float(PEER_TOPK)
            rb_ref[g, rows, :] = ranks[1].astype(rb_ref.dtype)
            select(g, sa_ref[g, rows, :], sb_ref[g, rows, :], ranks)

        for g in range(tile // LANES):
            pl.when(redo[g])(functools.partial(exact_group, g))
        return carry

    lax.fori_loop(0, PEER_HEADS, head, 0)


def _peer_route(x2d, wqt, keys, cand_ids):
    n = x2d.shape[0]
    tile = min(ROUTE_TILE, n)
    segs, _ = _candidate_tables()
    hk = PEER_HEADS * PEER_N_KEYS
    full = lambda a: pl.BlockSpec(a.shape, lambda i: (0,) * a.ndim)
    groups = tile // LANES
    slab = lambda r: pl.BlockSpec((groups, r, LANES), lambda i: (i, 0, 0))
    slab_shape = lambda r: (n // LANES, r, LANES)
    hs = PEER_HEADS * SUBLANES
    outs = ([((D_MODEL, n), BF16)] + [(slab_shape(hk), F32)] * 6 + [(slab_shape(hk), BF16)]
            + [(slab_shape(hs), F32)] * 3)
    out_specs = ([pl.BlockSpec((D_MODEL, tile), lambda i: (0, i))] + [slab(hk)] * 7 + [slab(hs)] * 3)
    return pl.pallas_call(
        functools.partial(_route_kernel, tile=tile, segs=segs),
        grid=(n // tile,),
        in_specs=[pl.BlockSpec((tile, D_MODEL), lambda i: (i, 0)), full(wqt), full(keys), full(cand_ids)],
        out_specs=out_specs,
        out_shape=[jax.ShapeDtypeStruct(s, d) for s, d in outs],
        scratch_shapes=[pltpu.VMEM((2 * PEER_HEADS * PEER_HALF, tile), BF16),
                        pltpu.VMEM((groups, 2, 24, LANES), F32),
                        pltpu.VMEM((groups, N_CAND, LANES), F32)],
        compiler_params=_params(("parallel",)),
        name="peer_route",
    )(x2d, wqt, keys, cand_ids)


def _twice_gelu(h):
    return h * (1.0 + lax.erf(h * (1.0 / math.sqrt(2.0))))


def _dense_kernel(xt_ref, u_ref, vt_ref, sa_ref, sb_ref, e1_ref, e2_ref, th_ref, ra_ref, rb_ref, tau_ref, ct_ref,
                  flag_ref, x_ref, g_ref, b_ref, y_ref, acc_ref, h_ref, w_ref, tied_ref, *, tile, etile):
    s = pl.program_id(1)
    n_tiles = pl.num_programs(1) - 1

    @pl.when(s == 0)
    def _():
        acc_ref[...] = jnp.zeros_like(acc_ref)
        tied_ref[0] = (jnp.max(flag_ref[...]) > 0.0).astype(jnp.int32)
        h_ref[0] = _dot(u_ref[...], xt_ref[...])

    tied = tied_ref[0] > 0

    def step(parity, exact):
        cur, nxt = 1 - parity, parity
        h_ref[nxt] = _dot(u_ref[...], xt_ref[...])
        down = None
        for r in range(etile // PEER_N_KEYS):
            i0 = (s - 1) * (etile // PEER_N_KEYS) + r
            for g in range(tile // LANES):
                lanes = slice(g * LANES, (g + 1) * LANES)
                row_of = lambda ref, h: jnp.broadcast_to(ref[g, pl.ds(h * PEER_N_KEYS + i0, 1), :], (SUBLANES, LANES))
                head_of = lambda ref, h: ref[g, h * SUBLANES:(h + 1) * SUBLANES, :]
                e1_b = [row_of(e1_ref, h) for h in range(PEER_HEADS)]
                if exact:
                    a_b = [row_of(sa_ref, h) for h in range(PEER_HEADS)]
                    ra_b = [row_of(ra_ref, h) for h in range(PEER_HEADS)]
                    tau_b = [head_of(tau_ref, h) for h in range(PEER_HEADS)]
                    ct_b = [head_of(ct_ref, h) for h in range(PEER_HEADS)]
                else:
                    th_b = [row_of(th_ref, h) for h in range(PEER_HEADS)]
                for jc in range(PEER_N_KEYS // GATE_ROWS):
                    if exact:
                        rb = [rb_ref[g, h * PEER_N_KEYS + jc * GATE_ROWS:h * PEER_N_KEYS + (jc + 1) * GATE_ROWS, :]
                              .astype(F32) for h in range(PEER_HEADS)]
                    parts = []
                    for sub in range(GATE_ROWS // SUBLANES):
                        j0 = jc * GATE_ROWS + sub * SUBLANES
                        gate = None
                        for h in range(PEER_HEADS):
                            rows = slice(h * PEER_N_KEYS + j0, h * PEER_N_KEYS + j0 + SUBLANES)
                            if exact:
                                total = a_b[h] + sb_ref[g, rows, :]
                                cand = ra_b[h] + rb[h][sub * SUBLANES:(sub + 1) * SUBLANES]
                                keep = (total > tau_b[h]) | ((total == tau_b[h]) & (cand <= ct_b[h]))
                            else:
                                keep = sb_ref[g, rows, :] >= th_b[h]
                            term = jnp.where(keep, e1_b[h] * e2_ref[g, rows, :], 0.0)
                            gate = term if gate is None else gate + term
                        hv = h_ref[cur, r * PEER_N_KEYS + j0:r * PEER_N_KEYS + j0 + SUBLANES, lanes]
                        parts.append(gate * _twice_gelu(hv))
                    rows = slice(r * PEER_N_KEYS + jc * GATE_ROWS, r * PEER_N_KEYS + (jc + 1) * GATE_ROWS)
                    w_ref[rows, lanes] = jnp.concatenate(parts, axis=0).astype(BF16)
            block = slice(r * PEER_N_KEYS, (r + 1) * PEER_N_KEYS)
            part = _dot(vt_ref[:, block], w_ref[block, :])
            down = part if down is None else down + part
        acc_ref[...] += down

    for parity in range(2):
        for exact in (False, True):
            cond = jnp.logical_and(jnp.logical_and(s >= 1, lax.rem(s, 2) == parity), tied == exact)
            pl.when(cond)(functools.partial(step, parity, exact))

    @pl.when(s == n_tiles)
    def _():
        y = DEEPNORM_ALPHA * x_ref[...] + acc_ref[...].T
        y_ref[...] = _layer_norm(y, g_ref[...], b_ref[...])


def _peer_dense(xt, u, vt, route, x2d, g, b):
    n = x2d.shape[0]
    tile = min(DENSE_TOKEN_TILE, n)
    etile = DENSE_EXPERT_TILE
    steps = PEER_N_EXPERTS // etile
    hk = PEER_HEADS * PEER_N_KEYS
    slab = lambda r: pl.BlockSpec((tile // LANES, r, LANES), lambda i, e: (i, 0, 0))
    full = lambda a: pl.BlockSpec(a.shape, lambda i, e: (0,) * a.ndim)
    return pl.pallas_call(
        functools.partial(_dense_kernel, tile=tile, etile=etile),
        grid=(n // tile, steps + 1),
        in_specs=[pl.BlockSpec((D_MODEL, tile), lambda i, e: (0, i)),
                  pl.BlockSpec((etile, D_MODEL), lambda i, e: (jnp.minimum(e, steps - 1), 0)),
                  pl.BlockSpec((D_MODEL, etile), lambda i, e: (0, jnp.maximum(e - 1, 0)))]
                 + [slab(hk)] * 7 + [slab(PEER_HEADS * SUBLANES)] * 3
                 + [pl.BlockSpec((tile, D_MODEL), lambda i, e: (i, 0)), full(g), full(b)],
        out_specs=pl.BlockSpec((tile, D_MODEL), lambda i, e: (i, 0)),
        out_shape=jax.ShapeDtypeStruct((n, D_MODEL), F32),
        scratch_shapes=[pltpu.VMEM((D_MODEL, tile), F32), pltpu.VMEM((2, etile, tile), F32),
                        pltpu.VMEM((etile, tile), BF16), pltpu.SMEM((1,), jnp.int32)],
        compiler_params=_params(("parallel", "arbitrary")),
        name="peer_dense",
    )(xt, u, vt, *route, x2d, g, b)


def _rope_tables(seq_len):
    pos = jnp.arange(seq_len, dtype=F32)
    inv_freq = ROPE_THETA ** (-jnp.arange(0, MLA_ROPE_DIM, 2, dtype=F32) / MLA_ROPE_DIM)
    ang = pos[:, None] * inv_freq[None, :]
    cos, sin = jnp.cos(ang), jnp.sin(ang)
    ones = jnp.ones((seq_len, MLA_NOPE_DIM), F32)
    pad = jnp.zeros((seq_len, LANES - MLA_QK_DIM), F32)
    cos_tab = jnp.concatenate([ones, cos, cos, pad], axis=1)
    sin_tab = jnp.concatenate([0.0 * ones, -sin, sin, pad], axis=1)
    return cos_tab, sin_tab


def _rotate_half_columns(w):
    half = MLA_ROPE_DIM // 2
    return jnp.concatenate([w[..., half:], w[..., :half]], axis=-1)


def _mixer_weights(w_in, w_uq, w_ukv):
    sb_scale = 1.0 / math.sqrt(SB_HEAD_DIM)
    wsb = jnp.concatenate([w_in[:, :D_SB] * sb_scale, w_in[:, D_SB:3 * D_SB]], axis=1).astype(BF16)
    wc = w_in[:, 3 * D_SB:3 * D_SB + MLA_Q_RANK + MLA_KV_RANK].astype(BF16)
    w_kr = w_in[:, 3 * D_SB + MLA_Q_RANK + MLA_KV_RANK:]
    lead = jnp.zeros((D_MODEL, MLA_NOPE_DIM), F32)
    trail = jnp.zeros((D_MODEL, LANES - MLA_QK_DIM), F32)
    wkr = jnp.concatenate([lead, w_kr, trail, lead, _rotate_half_columns(w_kr), trail], axis=1).astype(BF16)

    wq = w_uq.reshape(MLA_Q_RANK, MLA_HEADS, MLA_QK_DIM)
    qpad = jnp.zeros((MLA_Q_RANK, MLA_HEADS, LANES - MLA_QK_DIM), F32)
    q_plain = jnp.concatenate([wq, qpad], axis=2)
    q_rot = jnp.concatenate([0.0 * wq[..., :MLA_NOPE_DIM], _rotate_half_columns(wq[..., MLA_NOPE_DIM:]), qpad], axis=2)
    wq2 = jnp.concatenate([q_plain.reshape(MLA_Q_RANK, -1), q_rot.reshape(MLA_Q_RANK, -1)], axis=1).astype(BF16)

    wkv = w_ukv.reshape(MLA_KV_RANK, MLA_HEADS, MLA_NOPE_DIM + MLA_V_DIM)
    kpad = jnp.zeros((MLA_KV_RANK, MLA_HEADS, LANES - MLA_NOPE_DIM), F32)
    wk = jnp.concatenate([wkv[..., :MLA_NOPE_DIM], kpad], axis=2).reshape(MLA_KV_RANK, -1)
    wv = wkv[..., MLA_NOPE_DIM:].reshape(MLA_KV_RANK, -1)
    wkv2 = jnp.concatenate([wk, wv], axis=1).astype(BF16)
    return wsb, wc, wkr, wq2, wkv2


def kernel(x, w_in, g_cq, w_uq, g_ckv, w_ukv, w_o, ln1_g, ln1_b, peer_wq, peer_keys, peer_u, peer_v, ln2_g, ln2_b):
    b, s, d = x.shape
    n = b * s
    cos_tab, sin_tab = _rope_tables(s)
    tile = min(ATTN_TILE, s)
    strict_upper = (jnp.arange(tile)[:, None] > jnp.arange(tile)[None, :]).astype(BF16)
    _, cand_ids = _candidate_tables()
    cand_ids = jnp.asarray(np.broadcast_to(cand_ids[:, None], (N_CAND, LANES)).copy())

    x2d = x.reshape(n, d)
    for l in range(DEPTH):
        wsb, wc, wkr, wq2, wkv2 = _mixer_weights(w_in[l], w_uq[l], w_ukv[l])
        sbq, sbk, sbv, mq, mk, mvt = _in_proj(x2d, s, wsb, wc, wkr, g_cq[l][None, :], g_ckv[l][None, :], wq2, wkv2,
                                              cos_tab, sin_tab)
        sh = lambda a: a.reshape(b, s, a.shape[-1])
        o_sb = _sb_attention(sh(sbq), sh(sbk), sh(sbv), strict_upper)
        o_mla = _mla_attention(sh(mq), sh(mk), mvt)
        x2d = _out_proj_ln(o_sb.reshape(n, D_SB), o_mla.reshape(n, D_MLA_OUT), x2d, w_o[l].astype(BF16),
                           ln1_g[l][None, :], ln1_b[l][None, :])
        wqt = peer_wq[l].T.astype(BF16)
        keys = peer_keys[l].reshape(PEER_HEADS * 2, PEER_N_KEYS, PEER_HALF).astype(BF16)
        route = _peer_route(x2d, wqt, keys, cand_ids)
        x2d = _peer_dense(route[0], peer_u[l].astype(BF16), peer_v[l].T.astype(BF16), route[1:], x2d,
                          ln2_g[l][None, :], ln2_b[l][None, :])
    return x2d.reshape(b, s, d)
```
